```python
import math
import jax, jax.numpy as jnp
from jax import lax
import numpy as np

D_MODEL = 1024
BATCH = 2
SEQ = 8192
DEPTH = 2

CHUNK = 64
Q_BLOCK = 128
RMS_EPS = 1e-6
CONV_WIDTH = 4

MLA_HEADS = 4
MLA_NOPE = 128
MLA_ROPE = 64
MLA_V = 128
MLA_QK_DIM = MLA_NOPE + MLA_ROPE
MLA_Q_RANK = 512
MLA_KV_RANK = 256
MLA_WIDTH = MLA_HEADS * MLA_V
MLA_SCALE = MLA_QK_DIM ** -0.5
ROPE_THETA = 10000.0

LRU_WIDTH = 512
LRU_BLOCKS = 8
LRU_BLOCK = LRU_WIDTH // LRU_BLOCKS
LRU_C = 8.0

GDN_HEADS = 4
GDN_DK = 128
GDN_DV = 128
GDN_WIDTH = GDN_HEADS * GDN_DV

D_MIX = MLA_WIDTH + LRU_WIDTH + GDN_WIDTH
IN_WIDTHS = (MLA_Q_RANK, MLA_KV_RANK, MLA_ROPE, LRU_WIDTH, LRU_WIDTH, GDN_HEADS * GDN_DK, GDN_HEADS * GDN_DK, GDN_WIDTH, GDN_WIDTH, GDN_HEADS, GDN_HEADS)
D_IN = sum(IN_WIDTHS)

D_FF_DENSE = 2816
N_EXPERTS = 8
TOP_K = 2
D_FF_EXPERT = 3584
N_DENSE = (DEPTH + 1) // 2
N_MOE = DEPTH // 2

kernel_name = "hybrid_mla_rglru_gdn_moe_trunk"


def rmsnorm(x, gain):
    xf = x.astype(jnp.float32)
    y = xf * lax.rsqrt(jnp.mean(xf * xf, axis=-1, keepdims=True) + RMS_EPS)
    return (y * gain.astype(jnp.float32)).astype(x.dtype)


def l2norm(x):
    return x * lax.rsqrt(jnp.sum(x * x, axis=-1, keepdims=True) + RMS_EPS)


def causal_depthwise_conv(x, w):
    c = x.shape[-1]
    return lax.conv_general_dilated(x, w[:, None, :].astype(x.dtype), window_strides=(1,), padding=[(w.shape[0] - 1, 0)], dimension_numbers=('NWC', 'WIO', 'NWC'), feature_group_count=c)


def rope_tables(positions):
    inv_freq = ROPE_THETA ** (-jnp.arange(0, MLA_ROPE, 2, dtype=jnp.float32) / MLA_ROPE)
    ang = positions.astype(jnp.float32)[..., None] * inv_freq
    return jnp.cos(ang), jnp.sin(ang)


def apply_rope(x, cos, sin):
    half = x.shape[-1] // 2
    x1 = x[..., :half].astype(jnp.float32)
    x2 = x[..., half:].astype(jnp.float32)
    c = cos[:, :, None, :]
    s = sin[:, :, None, :]
    return jnp.concatenate([x1 * c - x2 * s, x2 * c + x1 * s], axis=-1).astype(x.dtype)


def chunk_causal_attention(q, k, v):
    b, s, h, dqk = q.shape
    n_blocks = s // Q_BLOCK
    q_blocks = jnp.moveaxis(q.reshape(b, n_blocks, Q_BLOCK, h, dqk), 1, 0)
    key_chunk = jnp.arange(s) // CHUNK

    def one_block(args):
        blk, q_blk = args
        scores = jnp.einsum('bqhd,bkhd->bhqk', q_blk, k, preferred_element_type=jnp.float32) * MLA_SCALE
        q_chunk = (blk * Q_BLOCK + jnp.arange(Q_BLOCK)) // CHUNK
        visible = q_chunk[:, None] >= key_chunk[None, :]
        probs = jax.nn.softmax(jnp.where(visible, scores, -jnp.inf), axis=-1)
        return jnp.einsum('bhqk,bkhd->bqhd', probs.astype(v.dtype), v)

    out = lax.map(one_block, (jnp.arange(n_blocks), q_blocks))
    return jnp.moveaxis(out, 0, 1).reshape(b, s, h, v.shape[-1])


def mla_mixer(c_q, c_kv, k_rope, cos, sin, q_norm, w_uq, kv_norm, w_ukv, q_head_norm, k_head_norm, out_norm):
    b, s, _ = c_q.shape
    q = (rmsnorm(c_q, q_norm) @ w_uq).reshape(b, s, MLA_HEADS, MLA_QK_DIM)
    kv = (rmsnorm(c_kv, kv_norm) @ w_ukv).reshape(b, s, MLA_HEADS, MLA_NOPE + MLA_V)
    k_nope, v = kv[..., :MLA_NOPE], kv[..., MLA_NOPE:]
    k_pe = jnp.broadcast_to(k_rope[:, :, None, :], (b, s, MLA_HEADS, MLA_ROPE))
    k = jnp.concatenate([k_nope, k_pe], axis=-1)
    q = rmsnorm(q, q_head_norm)
    k = rmsnorm(k, k_head_norm)
    q = jnp.concatenate([q[..., :MLA_NOPE], apply_rope(q[..., MLA_NOPE:], cos, sin)], axis=-1)
    k = jnp.concatenate([k[..., :MLA_NOPE], apply_rope(k[..., MLA_NOPE:], cos, sin)], axis=-1)
    o = chunk_causal_attention(q, k, v)
    return rmsnorm(o.reshape(b, s, MLA_WIDTH), out_norm)


def linear_recurrence(a, u):
    def combine(left, right):
        a_l, b_l = left
        a_r, b_r = right
        return a_l * a_r, a_r * b_l + b_r
    _, h = lax.associative_scan(combine, (a, u), axis=1)
    return h


def rglru_mixer(x_in, gate_in, conv_w, conv_b, w_a, b_a, w_x, b_x, lam, out_norm):
    b, s, _ = x_in.shape
    xc = (causal_depthwise_conv(x_in, conv_w) + conv_b).astype(jnp.float32)
    xg = xc.reshape(b, s, LRU_BLOCKS, LRU_BLOCK)
    r = jax.nn.sigmoid(jnp.einsum('bsgi,gij->bsgj', xg, w_a.astype(jnp.float32)).reshape(b, s, LRU_WIDTH) + b_a.astype(jnp.float32))
    i = jax.nn.sigmoid(jnp.einsum('bsgi,gij->bsgj', xg, w_x.astype(jnp.float32)).reshape(b, s, LRU_WIDTH) + b_x.astype(jnp.float32))
    log_a = -LRU_C * r * jax.nn.softplus(-lam.astype(jnp.float32))
    a = jnp.exp(log_a)
    u = jnp.sqrt(-jnp.expm1(2.0 * log_a)) * (i * xc)
    h = linear_recurrence(a, u)
    y = h * jax.nn.gelu(gate_in.astype(jnp.float32))
    return rmsnorm(y, out_norm).astype(x_in.dtype)


def chunk_gated_delta_rule(q, k, v, g, beta):
    b, s, h, dk = q.shape
    dv = v.shape[-1]
    nc = s // CHUNK

    def to_chunks(t):
        return jnp.moveaxis(t.reshape((b, nc, CHUNK) + t.shape[2:]), 3, 1)

    q, k, v, g, beta = map(to_chunks, (q * dk ** -0.5, k, v, g, beta))
    g = jnp.cumsum(g, axis=-1)
    idx = jnp.arange(CHUNK)
    causal = idx[:, None] >= idx[None, :]
    strict = idx[:, None] > idx[None, :]
    gdiff = g[..., :, None] - g[..., None, :]
    decay = jnp.where(causal, jnp.exp(jnp.where(causal, gdiff, 0.0)), 0.0)
    k_beta = k * beta[..., None]
    a_low = jnp.where(strict, jnp.einsum('bhncd,bhnsd->bhncs', k_beta, k) * decay, 0.0)
    rhs = jnp.concatenate([v * beta[..., None], k_beta * jnp.exp(g)[..., None]], axis=-1)
    sol = lax.linalg.triangular_solve(a_low, rhs, left_side=True, lower=True, unit_diagonal=True)
    u, w = sol[..., :dv], sol[..., dv:]
    attn_intra = jnp.einsum('bhncd,bhnsd->bhncs', q, k) * decay
    q_decayed = q * jnp.exp(g)[..., None]
    k_to_end = k * jnp.exp(g[..., -1:] - g)[..., None]
    chunk_decay = jnp.exp(g[..., -1])

    def step(state, inp):
        u_c, w_c, qd_c, att_c, ke_c, dec_c = inp
        v_new = u_c - jnp.einsum('bhcd,bhde->bhce', w_c, state)
        o_c = jnp.einsum('bhcd,bhde->bhce', qd_c, state) + jnp.einsum('bhcs,bhse->bhce', att_c, v_new)
        state = state * dec_c[..., None, None] + jnp.einsum('bhcd,bhce->bhde', ke_c, v_new)
        return state, o_c

    xs = tuple(jnp.moveaxis(t, 2, 0) for t in (u, w, q_decayed, attn_intra, k_to_end, chunk_decay))
    _, o = lax.scan(step, jnp.zeros((b, h, dk, dv), jnp.float32), xs)
    return jnp.transpose(jnp.moveaxis(o, 0, 2), (0, 2, 3, 1, 4)).reshape(b, s, h, dv)


def gdn_mixer(q_in, k_in, v_in, z_in, a_in, b_in, conv_w, a_log, dt_bias, out_norm):
    bsz, s, _ = q_in.shape
    qkv = jax.nn.silu(causal_depthwise_conv(jnp.concatenate([q_in, k_in, v_in], axis=-1), conv_w)).astype(jnp.float32)
    qd = GDN_HEADS * GDN_DK
    q = l2norm(qkv[..., :qd].reshape(bsz, s, GDN_HEADS, GDN_DK))
    k = l2norm(qkv[..., qd:2 * qd].reshape(bsz, s, GDN_HEADS, GDN_DK))
    v = qkv[..., 2 * qd:].reshape(bsz, s, GDN_HEADS, GDN_DV)
    beta = jax.nn.sigmoid(b_in.astype(jnp.float32))
    g = -jnp.exp(a_log.astype(jnp.float32)) * jax.nn.softplus(a_in.astype(jnp.float32) + dt_bias.astype(jnp.float32))
    o = chunk_gated_delta_rule(q, k, v, g, beta)
    z = z_in.astype(jnp.float32).reshape(bsz, s, GDN_HEADS, GDN_DV)
    o = rmsnorm(o, out_norm) * jax.nn.silu(z)
    return o.reshape(bsz, s, GDN_WIDTH).astype(q_in.dtype)


def swiglu(h, w_gate, w_up, w_down):
    return (jax.nn.silu(h @ w_gate) * (h @ w_up)) @ w_down


def moe_swiglu(h, router_w, w_gate, w_up, w_down):
    b, s, d = h.shape
    t = h.reshape(b * s, d)
    logits = (t @ router_w).astype(jnp.float32)
    top_val, top_idx = lax.top_k(logits, TOP_K)
    gates = jax.nn.softmax(top_val, axis=-1)
    combine = jnp.sum(jax.nn.one_hot(top_idx, N_EXPERTS, dtype=jnp.float32) * gates[..., None], axis=1)
    out = jnp.zeros_like(t)
    for e in range(N_EXPERTS):
        out = out + combine[:, e:e + 1].astype(t.dtype) * swiglu(t, w_gate[e], w_up[e], w_down[e])
    return out.reshape(b, s, d)


def setup_inputs(seed: int = 0) -> dict:
    key = jax.random.key(seed)
    ks = list(jax.random.split(key, 48))

    def nk():
        return ks.pop()

    def nrm(shape, scale):
        return scale * jax.random.normal(nk(), shape, jnp.float32)

    def gain(shape):
        return 1.0 + 0.02 * jax.random.normal(nk(), shape, jnp.float32)

    L = DEPTH
    x = jax.random.normal(nk(), (BATCH, SEQ, D_MODEL), jnp.float32)
    steps = jax.random.randint(nk(), (BATCH, SEQ), 1, 3, dtype=jnp.int32)
    offset = jax.random.randint(nk(), (BATCH, 1), 0, 4096, dtype=jnp.int32)
    positions = offset + jnp.cumsum(steps, axis=1, dtype=jnp.int32) - steps[:, :1]

    u_lam = jax.random.uniform(nk(), (L, LRU_WIDTH), jnp.float32, 0.9, 0.999)
    a0 = u_lam ** (1.0 / LRU_C)
    lru_lambda = jnp.log(a0) - jnp.log1p(-a0)
    dt = jnp.exp(jax.random.uniform(nk(), (L, GDN_HEADS), jnp.float32, math.log(1e-3), math.log(1e-1)))
    gdn_dt_bias = dt + jnp.log(-jnp.expm1(-dt))
    gdn_a_log = jnp.log(jax.random.uniform(nk(), (L, GDN_HEADS), jnp.float32, 1.0, 16.0))

    return {
        'x': x,
        'positions': positions,
        'mix_norm': gain((L, D_MODEL)),
        'w_in': nrm((L, D_MODEL, D_IN), D_MODEL ** -0.5),
        'mla_q_norm': gain((L, MLA_Q_RANK)),
        'mla_w_uq': nrm((L, MLA_Q_RANK, MLA_HEADS * MLA_QK_DIM), MLA_Q_RANK ** -0.5),
        'mla_kv_norm': gain((L, MLA_KV_RANK)),
        'mla_w_ukv': nrm((L, MLA_KV_RANK, MLA_HEADS * (MLA_NOPE + MLA_V)), MLA_KV_RANK ** -0.5),
        'mla_q_head_norm': gain((L, MLA_QK_DIM)),
        'mla_k_head_norm': gain((L, MLA_QK_DIM)),
        'mla_out_norm': gain((L, MLA_WIDTH)),
        'lru_conv_w': nrm((L, CONV_WIDTH, LRU_WIDTH), CONV_WIDTH ** -0.5),
        'lru_conv_b': nrm((L, LRU_WIDTH), 0.02),
        'lru_w_a': nrm((L, LRU_BLOCKS, LRU_BLOCK, LRU_BLOCK), LRU_BLOCK ** -0.5),
        'lru_b_a': nrm((L, LRU_WIDTH), 0.02),
        'lru_w_x': nrm((L, LRU_BLOCKS, LRU_BLOCK, LRU_BLOCK), LRU_BLOCK ** -0.5),
        'lru_b_x': nrm((L, LRU_WIDTH), 0.02),
        'lru_lambda': lru_lambda,
        'lru_out_norm': gain((L, LRU_WIDTH)),
        'gdn_conv_w': nrm((L, CONV_WIDTH, 2 * GDN_HEADS * GDN_DK + GDN_WIDTH), CONV_WIDTH ** -0.5),
        'gdn_a_log': gdn_a_log,
        'gdn_dt_bias': gdn_dt_bias,
        'gdn_out_norm': gain((L, GDN_DV)),
        'w_out': nrm((L, D_MIX, D_MODEL), D_MIX ** -0.5),
        'ffn_norm': gain((L, D_MODEL)),
        'dense_w_gate': nrm((N_DENSE, D_MODEL, D_FF_DENSE), D_MODEL ** -0.5),
        'dense_w_up': nrm((N_DENSE, D_MODEL, D_FF_DENSE), D_MODEL ** -0.5),
        'dense_w_down': nrm((N_DENSE, D_FF_DENSE, D_MODEL), D_FF_DENSE ** -0.5),
        'router_w': nrm((N_MOE, D_MODEL, N_EXPERTS), D_MODEL ** -0.5),
        'moe_w_gate': nrm((N_MOE, N_EXPERTS, D_MODEL, D_FF_EXPERT), D_MODEL ** -0.5),
        'moe_w_up': nrm((N_MOE, N_EXPERTS, D_MODEL, D_FF_EXPERT), D_MODEL ** -0.5),
        'moe_w_down': nrm((N_MOE, N_EXPERTS, D_FF_EXPERT, D_MODEL), D_FF_EXPERT ** -0.5),
    }


def reference(x, positions, mix_norm, w_in, mla_q_norm, mla_w_uq, mla_kv_norm, mla_w_ukv, mla_q_head_norm, mla_k_head_norm, mla_out_norm, lru_conv_w, lru_conv_b, lru_w_a, lru_b_a, lru_w_x, lru_b_x, lru_lambda, lru_out_norm, gdn_conv_w, gdn_a_log, gdn_dt_bias, gdn_out_norm, w_out, ffn_norm, dense_w_gate, dense_w_up, dense_w_down, router_w, moe_w_gate, moe_w_up, moe_w_down):
    cos, sin = rope_tables(positions)
    split_at = tuple(int(v) for v in np.cumsum(IN_WIDTHS)[:-1])
    for layer in range(DEPTH):
        h = rmsnorm(x, mix_norm[layer])
        (c_q, c_kv, k_rope, lru_x, lru_gate, g_q, g_k, g_v, g_z, g_a, g_b) = jnp.split(h @ w_in[layer], split_at, axis=-1)
        y_mla = mla_mixer(c_q, c_kv, k_rope, cos, sin, mla_q_norm[layer], mla_w_uq[layer], mla_kv_norm[layer], mla_w_ukv[layer], mla_q_head_norm[layer], mla_k_head_norm[layer], mla_out_norm[layer])
        y_lru = rglru_mixer(lru_x, lru_gate, lru_conv_w[layer], lru_conv_b[layer], lru_w_a[layer], lru_b_a[layer], lru_w_x[layer], lru_b_x[layer], lru_lambda[layer], lru_out_norm[layer])
        y_gdn = gdn_mixer(g_q, g_k, g_v, g_z, g_a, g_b, gdn_conv_w[layer], gdn_a_log[layer], gdn_dt_bias[layer], gdn_out_norm[layer])
        x = x + jnp.concatenate([y_mla, y_lru, y_gdn], axis=-1) @ w_out[layer]
        h = rmsnorm(x, ffn_norm[layer])
        if layer % 2 == 0:
            x = x + swiglu(h, dense_w_gate[layer // 2], dense_w_up[layer // 2], dense_w_down[layer // 2])
        else:
            x = x + moe_swiglu(h, router_w[layer // 2], moe_w_gate[layer // 2], moe_w_up[layer // 2], moe_w_down[layer // 2])
    return x
```

```python
import functools
import math

import jax
import jax.numpy as jnp
from jax import lax
from jax.experimental import pallas as pl
from jax.experimental.pallas import tpu as pltpu

F32 = jnp.float32
BF16 = jnp.bfloat16
HIGHEST = lax.Precision.HIGHEST

D_MODEL = 1024
CHUNK = 64
RMS_EPS = 1e-6
CONV_WIDTH = 4
MLA_HEADS = 4
MLA_NOPE = 128
MLA_ROPE = 64
MLA_V = 128
MLA_QK_DIM = MLA_NOPE + MLA_ROPE
MLA_Q_RANK = 512
MLA_KV_RANK = 256
MLA_WIDTH = MLA_HEADS * MLA_V
MLA_SCALE = MLA_QK_DIM ** -0.5
ROPE_THETA = 10000.0
LRU_WIDTH = 512
LRU_BLOCKS = 8
LRU_C = 8.0
GDN_HEADS = 4
GDN_DK = 128
GDN_DV = 128
GDN_WIDTH = GDN_HEADS * GDN_DV
N_EXPERTS = 8
TOP_K = 2

LANES = 128
SUBLANES = 8
QK_PAD = 256
VMEM_LIMIT = 56 * 1024 * 1024

W_MLA = MLA_Q_RANK + MLA_KV_RANK + LANES
W_LRU = 2 * LRU_WIDTH
W_QKV = 3 * GDN_WIDTH
W_Z = GDN_WIDTH
W_AB = LANES
IN_GROUPS = (W_MLA, W_LRU, W_QKV, W_Z, W_AB)

TM_PROJ = 256
TM_PREP = 512
TQ = 512
TK = 512
TS_LRU = 256
T_GDN = 128
TM_OUT = 512
TM_FFN = 512
FC_DENSE = 1408
TM_MOE = 1024
TF_MOE = 896
TC_COMB = 256


def _cparams(sem):
    return pltpu.CompilerParams(dimension_semantics=sem, vmem_limit_bytes=VMEM_LIMIT)


def _rms(x, gain):
    return x * lax.rsqrt(jnp.mean(x * x, axis=-1, keepdims=True) + RMS_EPS) * gain


def _sigmoid(x):
    return 1.0 / (1.0 + jnp.exp(-x))


def _softplus(x):
    return jnp.maximum(x, 0.0) + jnp.log1p(jnp.exp(-jnp.abs(x)))


def _dot(a, b, **kw):
    return jnp.dot(a, b, preferred_element_type=F32, **kw)


def _dot_nt(a, b):
    return lax.dot_general(a, b, (((1,), (1,)), ((), ())), preferred_element_type=F32)


def _dot_tn(a, b, **kw):
    return lax.dot_general(a, b, (((0,), (0,)), ((), ())), preferred_element_type=F32, **kw)


def _const_spec(shape):
    nd = len(shape)
    return pl.BlockSpec(shape, lambda *_: (0,) * nd)


def _inproj_body(x_ref, g_ref, w_ref, mla_ref, lru_ref, qkv_ref, z_ref, ab_ref):
    h = _rms(x_ref[...], g_ref[...]).astype(BF16)
    off = 0
    for ref in (mla_ref, lru_ref, qkv_ref, z_ref, ab_ref):
        width = ref.shape[-1]
        ref[...] = _dot(h, w_ref[:, off:off + width]).astype(ref.dtype)
        off += width


def _inproj(x, gain, w_pad):
    b, s, d = x.shape
    tm = min(TM_PROJ, s)
    dtypes = (BF16, BF16, BF16, BF16, F32)
    return pl.pallas_call(
        _inproj_body,
        grid=(b, s // tm),
        in_specs=[
            pl.BlockSpec((None, tm, d), lambda i, j: (i, j, 0)),
            _const_spec((1, d)),
            _const_spec(w_pad.shape),
        ],
        out_specs=[pl.BlockSpec((None, tm, w), lambda i, j: (i, j, 0)) for w in IN_GROUPS],
        out_shape=[jax.ShapeDtypeStruct((b, s, w), dt) for w, dt in zip(IN_GROUPS, dtypes)],
        compiler_params=_cparams(("parallel", "parallel")),
        name="inproj",
    )(x, gain, w_pad)


def _mla_prep_body(mla_ref, pos_ref, freq_ref, qn_ref, wuq_ref, kvn_ref, wukv_ref, qhn_ref, khn_ref,
                   q_ref, k_ref, v_ref):
    m = mla_ref[...].astype(F32)
    tm = m.shape[0]
    cq = m[:, :MLA_Q_RANK]
    ckv = m[:, MLA_Q_RANK:MLA_Q_RANK + MLA_KV_RANK]
    kr = m[:, MLA_Q_RANK + MLA_KV_RANK:]
    qall = _dot(_rms(cq, qn_ref[...]).astype(BF16), wuq_ref[...])
    kvall = _dot(_rms(ckv, kvn_ref[...]).astype(BF16), wukv_ref[...])

    ang = pos_ref[...].astype(F32) * freq_ref[...]
    cos = jnp.cos(ang)
    sin = jnp.sin(ang)
    lane = lax.broadcasted_iota(jnp.int32, (tm, LANES), 1)
    half = MLA_ROPE // 2
    first = lane < half
    sin_signed = jnp.where(first, -sin, sin)

    def rope(x):
        swapped = jnp.where(first, pltpu.roll(x, LANES - half, 1), pltpu.roll(x, half, 1))
        return x * cos + swapped * sin_signed

    qg = qhn_ref[...]
    kg = khn_ref[...]
    kr_ss = jnp.sum(kr * kr, axis=-1, keepdims=True)
    for h in range(MLA_HEADS):
        base = h * QK_PAD
        qn = qall[:, base:base + MLA_NOPE]
        qr = qall[:, base + MLA_NOPE:base + QK_PAD]
        ss = jnp.sum(qn * qn, axis=-1, keepdims=True) + jnp.sum(qr * qr, axis=-1, keepdims=True)
        r = lax.rsqrt(ss * (1.0 / MLA_QK_DIM) + RMS_EPS) * MLA_SCALE
        q_ref[h, :, :MLA_NOPE] = (qn * r * qg[:, :MLA_NOPE]).astype(BF16)
        q_ref[h, :, MLA_NOPE:] = rope(qr * r * qg[:, MLA_NOPE:]).astype(BF16)

        kn = kvall[:, base:base + MLA_NOPE]
        ss = jnp.sum(kn * kn, axis=-1, keepdims=True) + kr_ss
        r = lax.rsqrt(ss * (1.0 / MLA_QK_DIM) + RMS_EPS)
        k_ref[h, :, :MLA_NOPE] = (kn * r * kg[:, :MLA_NOPE]).astype(BF16)
        k_ref[h, :, MLA_NOPE:] = rope(kr * r * kg[:, MLA_NOPE:]).astype(BF16)
        v_ref[h] = kvall[:, base + MLA_NOPE:base + QK_PAD].astype(BF16)


def _mla_prep(mla_in, pos3, freq, qn, wuq, kvn, wukv, qhn, khn):
    b, s, _ = mla_in.shape
    tm = min(TM_PREP, s)
    hd = MLA_HEADS
    return pl.pallas_call(
        _mla_prep_body,
        grid=(b, s // tm),
        in_specs=[
            pl.BlockSpec((None, tm, W_MLA), lambda i, j: (i, j, 0)),
            pl.BlockSpec((None, tm, 1), lambda i, j: (i, j, 0)),
            _const_spec(freq.shape), _const_spec(qn.shape), _const_spec(wuq.shape),
            _const_spec(kvn.shape), _const_spec(wukv.shape), _const_spec(qhn.shape), _const_spec(khn.shape),
        ],
        out_specs=[
            pl.BlockSpec((None, hd, tm, QK_PAD), lambda i, j: (i, 0, j, 0)),
            pl.BlockSpec((None, hd, tm, QK_PAD), lambda i, j: (i, 0, j, 0)),
            pl.BlockSpec((None, hd, tm, MLA_V), lambda i, j: (i, 0, j, 0)),
        ],
        out_shape=[
            jax.ShapeDtypeStruct((b, hd, s, QK_PAD), BF16),
            jax.ShapeDtypeStruct((b, hd, s, QK_PAD), BF16),
            jax.ShapeDtypeStruct((b, hd, s, MLA_V), BF16),
        ],
        compiler_params=_cparams(("parallel", "parallel")),
        name="mla_prep",
    )(mla_in, pos3, freq, qn, wuq, kvn, wukv, qhn, khn)


NEG_BIG = -1e30


def _attn_body(q_ref, k_ref, v_ref, o_ref, *, tq, tk):
    i = pl.program_id(2)
    q = q_ref[...]

    def step(start, carry, mask):
        m, l, acc = carry
        kb = k_ref[pl.ds(start, tk), :]
        vb = v_ref[pl.ds(start, tk), :]
        s = _dot_nt(q, kb)
        if mask is not None:
            s = jnp.where(mask, s, NEG_BIG)
        m_new = jnp.maximum(m, jnp.max(s, axis=-1, keepdims=True))
        p = jnp.exp(s - m_new)
        alpha = jnp.exp(m - m_new)
        l = alpha * l + jnp.sum(p, axis=-1, keepdims=True)
        acc = alpha * acc + _dot(p.astype(BF16), vb)
        return m_new, l, acc

    carry = (jnp.full((tq, 1), NEG_BIG, F32), jnp.zeros((tq, 1), F32), jnp.zeros((tq, MLA_V), F32))
    n_sub = tq // tk
    carry = lax.fori_loop(0, i * n_sub, lambda j, c: step(pl.multiple_of(j * tk, tk), c, None), carry)
    row_chunk = lax.broadcasted_iota(jnp.int32, (tq, tk), 0) // CHUNK
    col_chunk = lax.broadcasted_iota(jnp.int32, (tq, tk), 1) // CHUNK
    for d in range(n_sub):
        mask = row_chunk >= col_chunk + (d * tk) // CHUNK
        carry = step(pl.multiple_of(i * tq + d * tk, tk), carry, mask)
    _, l, acc = carry
    o_ref[...] = (acc / l).astype(o_ref.dtype)


def _attention(q, k, v):
    b, hd, s, _ = q.shape
    tq = min(TQ, s)
    tk = min(TK, tq)
    return pl.pallas_call(
        functools.partial(_attn_body, tq=tq, tk=tk),
        grid=(b, hd, s // tq),
        in_specs=[
            pl.BlockSpec((None, None, tq, QK_PAD), lambda bi, h, i: (bi, h, i, 0)),
            pl.BlockSpec((None, None, s, QK_PAD), lambda bi, h, i: (bi, h, 0, 0)),
            pl.BlockSpec((None, None, s, MLA_V), lambda bi, h, i: (bi, h, 0, 0)),
        ],
        out_specs=pl.BlockSpec((None, tq, MLA_V), lambda bi, h, i: (bi, i, h)),
        out_shape=jax.ShapeDtypeStruct((b, s, MLA_WIDTH), BF16),
        compiler_params=_cparams(("parallel", "parallel", "arbitrary")),
        name="mla_attention",
    )(q, k, v)


def _causal_conv(x, halo_ref, w):
    t = x.shape[0]
    xe = jnp.concatenate([halo_ref[...], x], axis=0)
    halo_ref[...] = x[t - SUBLANES:, :]
    out = xe[SUBLANES:, :] * w[CONV_WIDTH - 1:CONV_WIDTH, :]
    for j in range(CONV_WIDTH - 1):
        shift = CONV_WIDTH - 1 - j
        out = out + xe[SUBLANES - shift:SUBLANES - shift + t, :] * w[j:j + 1, :]
    return out


def _lru_body(lru_ref, cw_ref, cb_ref, wa_ref, ba_ref, wx_ref, bx_ref, lam_ref, on_ref, y_ref, halo_ref, h_ref):
    @pl.when(pl.program_id(1) == 0)
    def _():
        halo_ref[...] = jnp.zeros_like(halo_ref)
        h_ref[...] = jnp.zeros_like(h_ref)

    blk = lru_ref[...].astype(F32)
    ts = blk.shape[0]
    x = blk[:, :LRU_WIDTH]
    gate = blk[:, LRU_WIDTH:]
    xc = _causal_conv(x, halo_ref, cw_ref[...]) + cb_ref[...]
    xb = xc.astype(BF16)
    r = _sigmoid(_dot(xb, wa_ref[...]) + ba_ref[...])
    ig = _sigmoid(_dot(xb, wx_ref[...]) + bx_ref[...])
    log_a = (-LRU_C) * r * _softplus(-lam_ref[...])
    a = jnp.exp(log_a)
    u = jnp.sqrt(-jnp.tanh(log_a) * (a * a + 1.0)) * (ig * xc)

    row = lax.broadcasted_iota(jnp.int32, (ts, LRU_WIDTH), 0)
    d = 1
    while d < ts:
        keep = row >= d
        a_prev = jnp.where(keep, pltpu.roll(a, d, 0), 1.0)
        u_prev = jnp.where(keep, pltpu.roll(u, d, 0), 0.0)
        u = a * u_prev + u
        a = a * a_prev
        d *= 2
    h = a * h_ref[...] + u
    h_ref[...] = h[ts - 1:ts, :]
    y = h * jax.nn.gelu(gate, approximate=True)
    y_ref[...] = _rms(y, on_ref[...]).astype(y_ref.dtype)


def _rglru(lru_in, cw, cb, wa, ba, wx, bx, lam, on):
    b, s, _ = lru_in.shape
    ts = min(TS_LRU, s)
    params = (cw, cb, wa, ba, wx, bx, lam, on)
    return pl.pallas_call(
        _lru_body,
        grid=(b, s // ts),
        in_specs=[pl.BlockSpec((None, ts, W_LRU), lambda i, j: (i, j, 0))] + [_const_spec(p.shape) for p in params],
        out_specs=pl.BlockSpec((None, ts, LRU_WIDTH), lambda i, j: (i, j, 0)),
        out_shape=jax.ShapeDtypeStruct((b, s, LRU_WIDTH), BF16),
        scratch_shapes=[pltpu.VMEM((SUBLANES, LRU_WIDTH), F32), pltpu.VMEM((1, LRU_WIDTH), F32)],
        compiler_params=_cparams(("parallel", "arbitrary")),
        name="rglru",
    )(lru_in, *params)


def _gdn_body(qkv_ref, z_ref, ab_ref, cw_ref, alog_ref, dtb_ref, on_ref, y_ref, halo_ref, state_ref):
    @pl.when(pl.program_id(1) == 0)
    def _():
        halo_ref[...] = jnp.zeros_like(halo_ref)
        state_ref[...] = jnp.zeros_like(state_ref)

    x = qkv_ref[...].astype(F32)
    t = x.shape[0]
    c = _causal_conv(x, halo_ref, cw_ref[...])
    c = c * _sigmoid(c)

    ab = ab_ref[...]
    lane = lax.broadcasted_iota(jnp.int32, (t, LANES), 1)
    g = jnp.where(lane < GDN_HEADS, -jnp.exp(alog_ref[...]) * _softplus(ab + dtb_ref[...]), 0.0)
    beta_all = _sigmoid(ab)

    ri = lax.broadcasted_iota(jnp.int32, (t, t), 0)
    ci = lax.broadcasted_iota(jnp.int32, (t, t), 1)
    same = (ri // CHUNK) == (ci // CHUNK)
    causal = jnp.logical_and(same, ri >= ci)
    strict = jnp.logical_and(same, ri > ci)
    one = jnp.ones((t, t), F32)
    zero = jnp.zeros((t, t), F32)
    gc = _dot(jnp.where(causal, one, zero), g, precision=HIGHEST)
    gct = _dot_tn(g, jnp.where(jnp.logical_and(same, ri <= ci), one, zero), precision=HIGHEST)
    gl = _dot(jnp.where(same, one, zero), g, precision=HIGHEST)
    eye = jnp.where(ri == ci, one, zero)

    on = on_ref[...]
    n_chunks = t // CHUNK
    for h in range(GDN_HEADS):
        q = c[:, h * GDN_DK:(h + 1) * GDN_DK]
        k = c[:, GDN_WIDTH + h * GDN_DK:GDN_WIDTH + (h + 1) * GDN_DK]
        v = c[:, 2 * GDN_WIDTH + h * GDN_DV:2 * GDN_WIDTH + (h + 1) * GDN_DV]
        q = q * (lax.rsqrt(jnp.sum(q * q, axis=-1, keepdims=True) + RMS_EPS) * (GDN_DK ** -0.5))
        k = k * lax.rsqrt(jnp.sum(k * k, axis=-1, keepdims=True) + RMS_EPS)
        beta = beta_all[:, GDN_HEADS + h:GDN_HEADS + h + 1]
        gcc = gc[:, h:h + 1]
        gcr = gct[h:h + 1, :]
        glc = gl[:, h:h + 1]
        decay = jnp.where(causal, jnp.exp(jnp.where(causal, gcc - gcr, 0.0)), 0.0)
        kb = k * beta
        k16 = k.astype(BF16)
        a_low = jnp.where(strict, _dot_nt(kb.astype(BF16), k16) * decay, 0.0)
        attn = _dot_nt(q.astype(BF16), k16) * decay

        inv = eye - a_low
        p = a_low
        for _ in range(int(math.log2(CHUNK)) - 1):
            p16 = p.astype(BF16)
            p = _dot(p16, p16)
            inv = inv + _dot(inv.astype(BF16), p.astype(BF16))

        eg = jnp.exp(gcc)
        rhs = jnp.concatenate([v * beta, kb * eg], axis=1).astype(BF16)
        sol = _dot(inv.astype(BF16), rhs)
        u = sol[:, :GDN_DV]
        w = sol[:, GDN_DV:].astype(BF16)
        qd = (q * eg).astype(BF16)
        ke = (k * jnp.exp(glc - gcc)).astype(BF16)
        dec = jnp.exp(glc)
        attn16 = attn.astype(BF16)

        state = state_ref[h]
        outs = []
        for n in range(n_chunks):
            lo, hi = n * CHUNK, (n + 1) * CHUNK
            s16 = state.astype(BF16)
            v_new = u[lo:hi] - _dot(w[lo:hi], s16)
            v16 = v_new.astype(BF16)
            outs.append(_dot(qd[lo:hi], s16) + _dot(attn16[lo:hi, lo:hi], v16))
            state = state * dec[lo:lo + 1, :] + _dot_tn(ke[lo:hi], v16)
        state_ref[h] = state
        o = jnp.concatenate(outs, axis=0)
        zz = z_ref[:, h * GDN_DV:(h + 1) * GDN_DV].astype(F32)
        y_ref[:, h * GDN_DV:(h + 1) * GDN_DV] = (_rms(o, on) * (zz * _sigmoid(zz))).astype(y_ref.dtype)


def _gdn(qkv, z, ab, cw, alog, dtb, on):
    b, s, _ = qkv.shape
    t = min(T_GDN, s)
    params = (cw, alog, dtb, on)
    return pl.pallas_call(
        _gdn_body,
        grid=(b, s // t),
        in_specs=[
            pl.BlockSpec((None, t, W_QKV), lambda i, j: (i, j, 0)),
            pl.BlockSpec((None, t, W_Z), lambda i, j: (i, j, 0)),
            pl.BlockSpec((None, t, W_AB), lambda i, j: (i, j, 0)),
        ] + [_const_spec(p.shape) for p in params],
        out_specs=pl.BlockSpec((None, t, GDN_WIDTH), lambda i, j: (i, j, 0)),
        out_shape=jax.ShapeDtypeStruct((b, s, GDN_WIDTH), BF16),
        scratch_shapes=[pltpu.VMEM((SUBLANES, W_QKV), F32), pltpu.VMEM((GDN_HEADS, GDN_DK, GDN_DV), F32)],
        compiler_params=_cparams(("parallel", "arbitrary")),
        name="gdn",
    )(qkv, z, ab, *params)


def _outproj_body(*refs, moe):
    if moe:
        x_ref, om_ref, yl_ref, yg_ref, mon_ref, wo_ref, fn_ref, rw_ref, xo_ref, h_ref, route_ref = refs
    else:
        x_ref, om_ref, yl_ref, yg_ref, mon_ref, wo_ref, fn_ref, xo_ref, h_ref = refs
    ym = _rms(om_ref[...].astype(F32), mon_ref[...]).astype(BF16)
    acc = _dot(ym, wo_ref[:MLA_WIDTH, :])
    acc = acc + _dot(yl_ref[...], wo_ref[MLA_WIDTH:MLA_WIDTH + LRU_WIDTH, :])
    acc = acc + _dot(yg_ref[...], wo_ref[MLA_WIDTH + LRU_WIDTH:, :])
    xn = x_ref[...] + acc
    xo_ref[...] = xn
    h = _rms(xn, fn_ref[...])
    h_ref[...] = h.astype(h_ref.dtype)
    if moe:
        tm = h.shape[0]
        logits = _dot(h, rw_ref[...], precision=HIGHEST)
        lane = lax.broadcasted_iota(jnp.int32, (tm, LANES), 1)
        logits = jnp.where(lane < N_EXPERTS, logits, -jnp.inf)
        m1 = jnp.max(logits, axis=-1, keepdims=True)
        i1 = jnp.min(jnp.where(logits == m1, lane, LANES), axis=-1, keepdims=True)
        rest = jnp.where(lane == i1, -jnp.inf, logits)
        m2 = jnp.max(rest, axis=-1, keepdims=True)
        i2 = jnp.min(jnp.where(rest == m2, lane, LANES), axis=-1, keepdims=True)
        e2 = jnp.exp(m2 - m1)
        g1 = 1.0 / (1.0 + e2)
        g2 = e2 * g1
        route = jnp.where(lane == 0, i1.astype(F32),
                          jnp.where(lane == 1, i2.astype(F32),
                                    jnp.where(lane == 2, g1, jnp.where(lane == 3, g2, 0.0))))
        route_ref[...] = route


def _outproj(x, om, yl, yg, mon, wo, fn, rw=None):
    b, s, d = x.shape
    tm = min(TM_OUT, s)
    moe = rw is not None
    row = lambda w: pl.BlockSpec((None, tm, w), lambda i, j: (i, j, 0))
    in_specs = [row(d), row(MLA_WIDTH), row(LRU_WIDTH), row(GDN_WIDTH),
                _const_spec(mon.shape), _const_spec(wo.shape), _const_spec(fn.shape)]
    args = [x, om, yl, yg, mon, wo, fn]
    out_specs = [row(d), row(d)]
    out_shape = [jax.ShapeDtypeStruct((b, s, d), F32), jax.ShapeDtypeStruct((b, s, d), F32 if moe else BF16)]
    if moe:
        in_specs.append(_const_spec(rw.shape))
        args.append(rw)
        out_specs.append(row(LANES))
        out_shape.append(jax.ShapeDtypeStruct((b, s, LANES), F32))
    return pl.pallas_call(
        functools.partial(_outproj_body, moe=moe),
        grid=(b, s // tm),
        in_specs=in_specs,
        out_specs=out_specs,
        out_shape=out_shape,
        compiler_params=_cparams(("parallel", "parallel")),
        name="outproj_moe" if moe else "outproj",
    )(*args)


def _dense_ffn_body(x_ref, h_ref, wg_ref, wu_ref, wd_ref, o_ref, *, fc):
    h = h_ref[...]
    acc = x_ref[...]
    for lo in range(0, wg_ref.shape[1], fc):
        g = _dot(h, wg_ref[:, lo:lo + fc])
        u = _dot(h, wu_ref[:, lo:lo + fc])
        a = (g * _sigmoid(g) * u).astype(BF16)
        acc = acc + _dot(a, wd_ref[lo:lo + fc, :])
    o_ref[...] = acc


def _dense_ffn(x, h, wg, wu, wd):
    n, d = x.shape
    tm = min(TM_FFN, n)
    ff = wg.shape[1]
    fc = FC_DENSE if ff % FC_DENSE == 0 else ff
    resident = lambda shape: pl.BlockSpec(shape, lambda i: (0, 0), pipeline_mode=pl.Buffered(1))
    return pl.pallas_call(
        functools.partial(_dense_ffn_body, fc=fc),
        grid=(n // tm,),
        in_specs=[
            pl.BlockSpec((tm, d), lambda i: (i, 0)),
            pl.BlockSpec((tm, d), lambda i: (i, 0)),
            resident(wg.shape), resident(wu.shape), resident(wd.shape),
        ],
        out_specs=pl.BlockSpec((tm, d), lambda i: (i, 0)),
        out_shape=jax.ShapeDtypeStruct((n, d), F32),
        compiler_params=_cparams(("parallel",)),
        name="dense_ffn",
    )(x, h, wg, wu, wd)


def _row_copy(src_hbm, dst_ref, src_row, dst_row, sem):
    return pltpu.make_async_copy(src_hbm.at[pl.ds(src_row, 1), :], dst_ref.at[pl.ds(dst_row, 1), :], sem)


def _moe_body(te_ref, tv_ref, rt_ref, h_hbm, wg_ref, wu_ref, wd_ref, gate_ref, y_ref, xg_ref, xb_ref, acc_ref, sem,
              *, tm, nf):
    del te_ref
    i = pl.program_id(0)
    f = pl.program_id(1)
    valid = tv_ref[i] > 0

    @pl.when(jnp.logical_and(valid, f == 0))
    def _gather():
        base = i * tm

        def issue(r, carry):
            _row_copy(h_hbm, xg_ref, rt_ref[base + r], r, sem).start()
            return carry

        def wait(r, carry):
            _row_copy(h_hbm, xg_ref, rt_ref[base + r], r, sem).wait()
            return carry

        lax.fori_loop(0, tm, issue, 0)
        lax.fori_loop(0, tm, wait, 0)
        xb_ref[...] = xg_ref[...].astype(BF16)
        acc_ref[...] = jnp.zeros_like(acc_ref)

    @pl.when(valid)
    def _compute():
        x = xb_ref[...]
        g = _dot(x, wg_ref[...])
        u = _dot(x, wu_ref[...])
        a = (g * _sigmoid(g) * u).astype(BF16)
        acc_ref[...] += _dot(a, wd_ref[...])

    @pl.when(f == nf - 1)
    def _finish():
        y_ref[...] = jnp.where(valid, acc_ref[...] * gate_ref[...], 0.0)


def _moe_experts(tile_expert, tile_valid, row_token, row_gate, h, wg, wu, wd, tm):
    n, d = h.shape
    n_tiles = tile_expert.shape[0]
    ff = wg.shape[2]
    tf = TF_MOE if ff % TF_MOE == 0 else ff
    nf = ff // tf

    def f_idx(i, f, tv):
        return jnp.where(tv[i] > 0, f, nf - 1)

    grid_spec = pltpu.PrefetchScalarGridSpec(
        num_scalar_prefetch=3,
        grid=(n_tiles, nf),
        in_specs=[
            pl.BlockSpec(memory_space=pl.ANY),
            pl.BlockSpec((None, d, tf), lambda i, f, te, tv, rt: (te[i], 0, f_idx(i, f, tv))),
            pl.BlockSpec((None, d, tf), lambda i, f, te, tv, rt: (te[i], 0, f_idx(i, f, tv))),
            pl.BlockSpec((None, tf, d), lambda i, f, te, tv, rt: (te[i], f_idx(i, f, tv), 0)),
            pl.BlockSpec((tm, 1), lambda i, f, te, tv, rt: (i, 0)),
        ],
        out_specs=pl.BlockSpec((tm, d), lambda i, f, te, tv, rt: (i, 0)),
        scratch_shapes=[pltpu.VMEM((tm, d), F32), pltpu.VMEM((tm, d), BF16), pltpu.VMEM((tm, d), F32),
                        pltpu.SemaphoreType.DMA(())],
    )
    return pl.pallas_call(
        functools.partial(_moe_body, tm=tm, nf=nf),
        grid_spec=grid_spec,
        out_shape=jax.ShapeDtypeStruct((n_tiles * tm, d), F32),
        compiler_params=_cparams(("arbitrary", "arbitrary")),
        name="moe_experts",
    )(tile_expert, tile_valid, row_token, h, wg, wu, wd, row_gate)


def _combine_body(pos_ref, x_ref, y_hbm, o_ref, buf_ref, sem, *, tc):
    base = pl.program_id(0) * (tc * TOP_K)

    def issue(r, carry):
        for slot in range(TOP_K):
            _row_copy(y_hbm, buf_ref.at[slot], pos_ref[base + TOP_K * r + slot], r, sem).start()
        return carry

    def wait(r, carry):
        for slot in range(TOP_K):
            _row_copy(y_hbm, buf_ref.at[slot], pos_ref[base + TOP_K * r + slot], r, sem).wait()
        return carry

    lax.fori_loop(0, tc, issue, 0)
    lax.fori_loop(0, tc, wait, 0)
    acc = x_ref[...]
    for slot in range(TOP_K):
        acc = acc + buf_ref[slot]
    o_ref[...] = acc


def _moe_combine(pos_flat, x, y_sorted):
    n, d = x.shape
    tc = min(TC_COMB, n)
    grid_spec = pltpu.PrefetchScalarGridSpec(
        num_scalar_prefetch=1,
        grid=(n // tc,),
        in_specs=[pl.BlockSpec((tc, d), lambda i, pos: (i, 0)), pl.BlockSpec(memory_space=pl.ANY)],
        out_specs=pl.BlockSpec((tc, d), lambda i, pos: (i, 0)),
        scratch_shapes=[pltpu.VMEM((TOP_K, tc, d), F32), pltpu.SemaphoreType.DMA(())],
    )
    return pl.pallas_call(
        functools.partial(_combine_body, tc=tc),
        grid_spec=grid_spec,
        out_shape=jax.ShapeDtypeStruct((n, d), F32),
        compiler_params=_cparams(("arbitrary",)),
        name="moe_combine",
    )(pos_flat, x, y_sorted)


def _moe_ffn(x, h, route, wg, wu, wd):
    n, d = x.shape
    tm = min(TM_MOE, n)
    experts = route[:, :TOP_K].astype(jnp.int32).reshape(-1)
    gates = route[:, TOP_K:2 * TOP_K].reshape(-1)
    onehot = (experts[:, None] == jnp.arange(N_EXPERTS, dtype=jnp.int32)[None, :]).astype(jnp.int32)
    rank = jnp.sum((jnp.cumsum(onehot, axis=0) - onehot) * onehot, axis=1)
    counts = jnp.sum(onehot, axis=0)
    padded = ((counts + tm - 1) // tm) * tm
    ends = jnp.cumsum(padded)
    starts = ends - padded
    pos = starts[experts] + rank
    n_tiles = (n * TOP_K) // tm + N_EXPERTS
    tile_start = jnp.arange(n_tiles, dtype=jnp.int32) * tm
    tile_valid = (tile_start < ends[-1]).astype(jnp.int32)
    tile_expert = jnp.minimum(jnp.sum((tile_start[:, None] >= ends[None, :]).astype(jnp.int32), axis=1), N_EXPERTS - 1)
    last_expert = jnp.max(jnp.where(counts > 0, jnp.arange(N_EXPERTS, dtype=jnp.int32), 0))
    tile_expert = jnp.where(tile_valid > 0, tile_expert, last_expert).astype(jnp.int32)
    token = jnp.arange(n * TOP_K, dtype=jnp.int32) // TOP_K
    row_token = jnp.zeros((n_tiles * tm,), jnp.int32).at[pos].set(token)
    row_gate = jnp.zeros((n_tiles * tm,), F32).at[pos].set(gates).reshape(-1, 1)
    y_sorted = _moe_experts(tile_expert, tile_valid, row_token, row_gate, h, wg, wu, wd, tm)
    return _moe_combine(pos.astype(jnp.int32), x, y_sorted)


def _pad_cols(w, width):
    return jnp.pad(w, ((0, 0), (0, width - w.shape[1])))


def _prep_w_in(w):
    n_mla = MLA_Q_RANK + MLA_KV_RANK + MLA_ROPE
    n_main = n_mla + W_LRU + W_QKV + W_Z
    return jnp.concatenate([_pad_cols(w[:, :n_mla], W_MLA), w[:, n_mla:n_main], _pad_cols(w[:, n_main:], W_AB)],
                           axis=1).astype(BF16)


def _prep_w_uq(w):
    w = w.reshape(MLA_Q_RANK, MLA_HEADS, MLA_QK_DIM)
    w = jnp.pad(w, ((0, 0), (0, 0), (0, QK_PAD - MLA_QK_DIM)))
    return w.reshape(MLA_Q_RANK, MLA_HEADS * QK_PAD).astype(BF16)


def _block_diag(w):
    g, bi, bo = w.shape
    eye = jnp.eye(g, dtype=w.dtype)
    return (eye[:, None, :, None] * w[:, :, None, :]).reshape(g * bi, g * bo).astype(BF16)


def _row(v, width=None):
    v = v.reshape(1, -1).astype(F32)
    return v if width is None else _pad_cols(v, width)


def kernel(x, positions, mix_norm, w_in, mla_q_norm, mla_w_uq, mla_kv_norm, mla_w_ukv, mla_q_head_norm, mla_k_head_norm, mla_out_norm, lru_conv_w, lru_conv_b, lru_w_a, lru_b_a, lru_w_x, lru_b_x, lru_lambda, lru_out_norm, gdn_conv_w, gdn_a_log, gdn_dt_bias, gdn_out_norm, w_out, ffn_norm, dense_w_gate, dense_w_up, dense_w_down, router_w, moe_w_gate, moe_w_up, moe_w_down):
    b, s, d = x.shape
    depth = w_in.shape[0]
    pos3 = positions.reshape(b, s, 1)
    inv_freq = ROPE_THETA ** (-jnp.arange(0, MLA_ROPE, 2, dtype=F32) / MLA_ROPE)
    freq = _row(jnp.concatenate([inv_freq, inv_freq]), LANES)

    for layer in range(depth):
        mla_in, lru_in, qkv_in, z_in, ab_in = _inproj(x, _row(mix_norm[layer]), _prep_w_in(w_in[layer]))

        q, k, v = _mla_prep(
            mla_in, pos3, freq, _row(mla_q_norm[layer]), _prep_w_uq(mla_w_uq[layer]), _row(mla_kv_norm[layer]),
            mla_w_ukv[layer].astype(BF16), _row(mla_q_head_norm[layer], QK_PAD), _row(mla_k_head_norm[layer], QK_PAD))
        o_mla = _attention(q, k, v)

        y_lru = _rglru(
            lru_in, lru_conv_w[layer], _row(lru_conv_b[layer]), _block_diag(lru_w_a[layer]), _row(lru_b_a[layer]),
            _block_diag(lru_w_x[layer]), _row(lru_b_x[layer]), _row(lru_lambda[layer]), _row(lru_out_norm[layer]))

        y_gdn = _gdn(qkv_in, z_in, ab_in, gdn_conv_w[layer], _row(gdn_a_log[layer], LANES),
                     _row(gdn_dt_bias[layer], LANES), _row(gdn_out_norm[layer]))

        wo = w_out[layer].astype(BF16)
        if layer % 2 == 0:
            x, h = _outproj(x, o_mla, y_lru, y_gdn, _row(mla_out_norm[layer]), wo, _row(ffn_norm[layer]))
            e = layer // 2
            x = _dense_ffn(x.reshape(b * s, d), h.reshape(b * s, d), dense_w_gate[e].astype(BF16),
                           dense_w_up[e].astype(BF16), dense_w_down[e].astype(BF16)).reshape(b, s, d)
        else:
            e = layer // 2
            x, h, route = _outproj(x, o_mla, y_lru, y_gdn, _row(mla_out_norm[layer]), wo, _row(ffn_norm[layer]),
                                   _pad_cols(router_w[e].astype(F32), LANES))
            x = _moe_ffn(x.reshape(b * s, d), h.reshape(b * s, d), route.reshape(b * s, LANES),
                         moe_w_gate[e].astype(BF16), moe_w_up[e].astype(BF16),
                         moe_w_down[e].astype(BF16)).reshape(b, s, d)
    return x
```

```python
import functools
import math

import jax
import jax.numpy as jnp
from jax import lax
from jax.experimental import pallas as pl
from jax.experimental.pallas import tpu as pltpu

F32 = jnp.float32
BF16 = jnp.bfloat16
HIGHEST = lax.Precision.HIGHEST

D_MODEL = 1024
CHUNK = 64
RMS_EPS = 1e-6
CONV_WIDTH = 4
MLA_HEADS = 4
MLA_NOPE = 128
MLA_ROPE = 64
MLA_V = 128
MLA_QK_DIM = MLA_NOPE + MLA_ROPE
MLA_Q_RANK = 512
MLA_KV_RANK = 256
MLA_WIDTH = MLA_HEADS * MLA_V
MLA_SCALE = MLA_QK_DIM ** -0.5
ROPE_THETA = 10000.0
LRU_WIDTH = 512
LRU_BLOCKS = 8
LRU_C = 8.0
GDN_HEADS = 4
GDN_DK = 128
GDN_DV = 128
GDN_WIDTH = GDN_HEADS * GDN_DV
N_EXPERTS = 8
TOP_K = 2

LANES = 128
SUBLANES = 8
QK_PAD = 256
V_PAD = 256
LOG2_E = math.log2(math.e)
VMEM_LIMIT = 56 * 1024 * 1024

W_MLA = MLA_Q_RANK + MLA_KV_RANK + LANES
W_LRU = 2 * LRU_WIDTH
W_QKV = 3 * GDN_WIDTH
W_Z = GDN_WIDTH
W_AB = LANES
IN_GROUPS = (W_MLA, W_LRU, W_QKV, W_Z, W_AB)

TM_PROJ = 256
TM_PREP = 512
TQ = 2048
TC_ATTN = 512
TK_MAIN = 2048
TS_LRU = 256
T_GDN = 256
TM_OUT = 512
TM_FFN = 512
FC_DENSE = 1408
TM_MOE = 1024
TF_MOE = 896
TC_COMB = 256


def _cparams(sem):
    return pltpu.CompilerParams(dimension_semantics=sem, vmem_limit_bytes=VMEM_LIMIT)


def _rms(x, gain):
    return x * lax.rsqrt(jnp.mean(x * x, axis=-1, keepdims=True) + RMS_EPS) * gain


def _sigmoid(x):
    return 1.0 / (1.0 + jnp.exp(-x))


def _softplus(x):
    return jnp.maximum(x, 0.0) + jnp.log1p(jnp.exp(-jnp.abs(x)))


def _dot(a, b, **kw):
    return jnp.dot(a, b, preferred_element_type=F32, **kw)


def _dot_nt(a, b):
    return lax.dot_general(a, b, (((1,), (1,)), ((), ())), preferred_element_type=F32)


def _dot_tn(a, b, **kw):
    return lax.dot_general(a, b, (((0,), (0,)), ((), ())), preferred_element_type=F32, **kw)


def _const_spec(shape):
    nd = len(shape)
    return pl.BlockSpec(shape, lambda *_: (0,) * nd)


def _inproj_body(x_ref, g_ref, w_ref, mla_ref, lru_ref, qkv_ref, z_ref, ab_ref):
    h = _rms(x_ref[...], g_ref[...]).astype(BF16)
    off = 0
    for ref in (mla_ref, lru_ref, qkv_ref, z_ref, ab_ref):
        width = ref.shape[-1]
        ref[...] = _dot(h, w_ref[:, off:off + width]).astype(ref.dtype)
        off += width


def _inproj(x, gain, w_pad):
    b, s, d = x.shape
    tm = min(TM_PROJ, s)
    dtypes = (BF16, BF16, BF16, BF16, F32)
    return pl.pallas_call(
        _inproj_body,
        grid=(b, s // tm),
        in_specs=[
            pl.BlockSpec((None, tm, d), lambda i, j: (i, j, 0)),
            _const_spec((1, d)),
            _const_spec(w_pad.shape),
        ],
        out_specs=[pl.BlockSpec((None, tm, w), lambda i, j: (i, j, 0)) for w in IN_GROUPS],
        out_shape=[jax.ShapeDtypeStruct((b, s, w), dt) for w, dt in zip(IN_GROUPS, dtypes)],
        compiler_params=_cparams(("parallel", "parallel")),
        name="inproj",
    )(x, gain, w_pad)


def _mla_prep_body(mla_ref, pos_ref, freq_ref, qn_ref, wuq_ref, kvn_ref, wukv_ref, qhn_ref, khn_ref,
                   q_ref, k_ref, v_ref):
    m = mla_ref[...].astype(F32)
    tm = m.shape[0]
    cq = m[:, :MLA_Q_RANK]
    ckv = m[:, MLA_Q_RANK:MLA_Q_RANK + MLA_KV_RANK]
    kr = m[:, MLA_Q_RANK + MLA_KV_RANK:]
    qall = _dot(_rms(cq, qn_ref[...]).astype(BF16), wuq_ref[...])
    kvall = _dot(_rms(ckv, kvn_ref[...]).astype(BF16), wukv_ref[...])

    ang = pos_ref[...].astype(F32) * freq_ref[...]
    cos = jnp.cos(ang)
    sin = jnp.sin(ang)
    lane = lax.broadcasted_iota(jnp.int32, (tm, LANES), 1)
    half = MLA_ROPE // 2
    first = lane < half
    sin_signed = jnp.where(first, -sin, sin)

    def rope(x):
        swapped = jnp.where(first, pltpu.roll(x, LANES - half, 1), pltpu.roll(x, half, 1))
        return x * cos + swapped * sin_signed

    qg = qhn_ref[...]
    kg = khn_ref[...]
    kr_ss = jnp.sum(kr * kr, axis=-1, keepdims=True)
    ones_col = jnp.where(lane == 0, 1.0, 0.0).astype(BF16)
    for h in range(MLA_HEADS):
        base = h * QK_PAD
        qn = qall[:, base:base + MLA_NOPE]
        qr = qall[:, base + MLA_NOPE:base + QK_PAD]
        ss = jnp.sum(qn * qn, axis=-1, keepdims=True) + jnp.sum(qr * qr, axis=-1, keepdims=True)
        r = lax.rsqrt(ss * (1.0 / MLA_QK_DIM) + RMS_EPS) * (MLA_SCALE * LOG2_E)
        q_ref[h, :, :MLA_NOPE] = (qn * r * qg[:, :MLA_NOPE]).astype(BF16)
        q_ref[h, :, MLA_NOPE:] = rope(qr * r * qg[:, MLA_NOPE:]).astype(BF16)

        kn = kvall[:, base:base + MLA_NOPE]
        ss = jnp.sum(kn * kn, axis=-1, keepdims=True) + kr_ss
        r = lax.rsqrt(ss * (1.0 / MLA_QK_DIM) + RMS_EPS)
        k_ref[h, :, :MLA_NOPE] = (kn * r * kg[:, :MLA_NOPE]).astype(BF16)
        k_ref[h, :, MLA_NOPE:] = rope(kr * r * kg[:, MLA_NOPE:]).astype(BF16)
        v_ref[h, :, :MLA_V] = kvall[:, base + MLA_NOPE:base + QK_PAD].astype(BF16)
        v_ref[h, :, MLA_V:] = ones_col


def _mla_prep(mla_in, pos3, freq, qn, wuq, kvn, wukv, qhn, khn):
    b, s, _ = mla_in.shape
    tm = min(TM_PREP, s)
    hd = MLA_HEADS
    return pl.pallas_call(
        _mla_prep_body,
        grid=(b, s // tm),
        in_specs=[
            pl.BlockSpec((None, tm, W_MLA), lambda i, j: (i, j, 0)),
            pl.BlockSpec((None, tm, 1), lambda i, j: (i, j, 0)),
            _const_spec(freq.shape), _const_spec(qn.shape), _const_spec(wuq.shape),
            _const_spec(kvn.shape), _const_spec(wukv.shape), _const_spec(qhn.shape), _const_spec(khn.shape),
        ],
        out_specs=[
            pl.BlockSpec((None, hd, tm, QK_PAD), lambda i, j: (i, 0, j, 0)),
            pl.BlockSpec((None, hd, tm, QK_PAD), lambda i, j: (i, 0, j, 0)),
            pl.BlockSpec((None, hd, tm, V_PAD), lambda i, j: (i, 0, j, 0)),
        ],
        out_shape=[
            jax.ShapeDtypeStruct((b, hd, s, QK_PAD), BF16),
            jax.ShapeDtypeStruct((b, hd, s, QK_PAD), BF16),
            jax.ShapeDtypeStruct((b, hd, s, V_PAD), BF16),
        ],
        compiler_params=_cparams(("parallel", "parallel")),
        name="mla_prep",
    )(mla_in, pos3, freq, qn, wuq, kvn, wukv, qhn, khn)


NEG_BIG = -1e30


def _attn_body(q_ref, k_ref, v_ref, o_ref, *, tq, tc, tkm):
    i = pl.program_id(2)
    n_chain = tq // tc
    qs = [q_ref[c * tc:(c + 1) * tc, :] for c in range(n_chain)]

    def steps(chains, starts, sizes, carries, mask):
        scores = [_dot_nt(qs[c], k_ref[pl.ds(st, sz), :]) for c, st, sz in zip(chains, starts, sizes)]
        probs, m_news = [], []
        for c, s in zip(chains, scores):
            if mask is not None:
                s = jnp.where(mask, s, NEG_BIG)
            m_new = jnp.maximum(carries[c][0], jnp.max(s, axis=-1, keepdims=True))
            probs.append(jnp.exp2(s - m_new).astype(BF16))
            m_news.append(m_new)
        out = list(carries)
        for c, st, sz, p, m_new in zip(chains, starts, sizes, probs, m_news):
            m, acc = carries[c]
            out[c] = (m_new, jnp.exp2(m - m_new) * acc + _dot(p, v_ref[pl.ds(st, sz), :]))
        return out

    all_chains = tuple(range(n_chain))

    def full_blocks(j, carries):
        start = pl.multiple_of(j * tkm, tkm)
        return tuple(steps(all_chains, (start,) * n_chain, (tkm,) * n_chain, carries, None))

    init = tuple((jnp.full((tc, 1), NEG_BIG, F32), jnp.zeros((tc, V_PAD), F32)) for _ in range(n_chain))
    carries = list(lax.fori_loop(0, i * (tq // tkm), full_blocks, init))
    tile_start = pl.multiple_of(i * tq, tq)
    if n_chain > 1:
        later = all_chains[1:]
        carries = steps(later, (tile_start,) * len(later), tuple(c * tc for c in later), carries, None)
    diag_mask = (lax.broadcasted_iota(jnp.int32, (tc, tc), 0) // CHUNK
                 >= lax.broadcasted_iota(jnp.int32, (tc, tc), 1) // CHUNK)
    carries = steps(all_chains, tuple(pl.multiple_of(tile_start + c * tc, tc) for c in all_chains),
                    (tc,) * n_chain, carries, diag_mask)
    for c in range(n_chain):
        acc = carries[c][1]
        o_ref[c * tc:(c + 1) * tc, :] = (acc[:, :MLA_V] / acc[:, MLA_V:MLA_V + 1]).astype(o_ref.dtype)


def _attention(q, k, v):
    b, hd, s, _ = q.shape
    tq = min(TQ, s)
    tc = min(TC_ATTN, tq)
    tkm = min(TK_MAIN, tq)
    return pl.pallas_call(
        functools.partial(_attn_body, tq=tq, tc=tc, tkm=tkm),
        grid=(b, hd, s // tq),
        in_specs=[
            pl.BlockSpec((None, None, tq, QK_PAD), lambda bi, h, i: (bi, h, i, 0)),
            pl.BlockSpec((None, None, s, QK_PAD), lambda bi, h, i: (bi, h, 0, 0)),
            pl.BlockSpec((None, None, s, V_PAD), lambda bi, h, i: (bi, h, 0, 0)),
        ],
        out_specs=pl.BlockSpec((None, tq, MLA_V), lambda bi, h, i: (bi, i, h)),
        out_shape=jax.ShapeDtypeStruct((b, s, MLA_WIDTH), BF16),
        compiler_params=_cparams(("parallel", "parallel", "arbitrary")),
        name="mla_attention",
    )(q, k, v)


def _causal_conv(x, halo_ref, w):
    t = x.shape[0]
    xe = jnp.concatenate([halo_ref[...], x], axis=0)
    halo_ref[...] = x[t - SUBLANES:, :]
    out = xe[SUBLANES:, :] * w[CONV_WIDTH - 1:CONV_WIDTH, :]
    for j in range(CONV_WIDTH - 1):
        shift = CONV_WIDTH - 1 - j
        out = out + xe[SUBLANES - shift:SUBLANES - shift + t, :] * w[j:j + 1, :]
    return out


def _lru_body(lru_ref, cw_ref, cb_ref, wa_ref, ba_ref, wx_ref, bx_ref, lam_ref, on_ref, y_ref, halo_ref, h_ref):
    @pl.when(pl.program_id(1) == 0)
    def _():
        halo_ref[...] = jnp.zeros_like(halo_ref)
        h_ref[...] = jnp.zeros_like(h_ref)

    blk = lru_ref[...].astype(F32)
    ts = blk.shape[0]
    x = blk[:, :LRU_WIDTH]
    gate = blk[:, LRU_WIDTH:]
    xc = _causal_conv(x, halo_ref, cw_ref[...]) + cb_ref[...]
    xb = xc.astype(BF16)
    r = _sigmoid(_dot(xb, wa_ref[...]) + ba_ref[...])
    ig = _sigmoid(_dot(xb, wx_ref[...]) + bx_ref[...])
    log_a = (-LRU_C) * r * _softplus(-lam_ref[...])
    a = jnp.exp(log_a)
    u = jnp.sqrt(-jnp.tanh(log_a) * (a * a + 1.0)) * (ig * xc)

    row = lax.broadcasted_iota(jnp.int32, (ts, LRU_WIDTH), 0)
    d = 1
    while d < ts:
        keep = row >= d
        a_prev = jnp.where(keep, pltpu.roll(a, d, 0), 1.0)
        u_prev = jnp.where(keep, pltpu.roll(u, d, 0), 0.0)
        u = a * u_prev + u
        a = a * a_prev
        d *= 2
    h = a * h_ref[...] + u
    h_ref[...] = h[ts - 1:ts, :]
    y = h * jax.nn.gelu(gate, approximate=True)
    y_ref[...] = _rms(y, on_ref[...]).astype(y_ref.dtype)


def _rglru(lru_in, cw, cb, wa, ba, wx, bx, lam, on):
    b, s, _ = lru_in.shape
    ts = min(TS_LRU, s)
    params = (cw, cb, wa, ba, wx, bx, lam, on)
    return pl.pallas_call(
        _lru_body,
        grid=(b, s // ts),
        in_specs=[pl.BlockSpec((None, ts, W_LRU), lambda i, j: (i, j, 0))] + [_const_spec(p.shape) for p in params],
        out_specs=pl.BlockSpec((None, ts, LRU_WIDTH), lambda i, j: (i, j, 0)),
        out_shape=jax.ShapeDtypeStruct((b, s, LRU_WIDTH), BF16),
        scratch_shapes=[pltpu.VMEM((SUBLANES, LRU_WIDTH), F32), pltpu.VMEM((1, LRU_WIDTH), F32)],
        compiler_params=_cparams(("parallel", "arbitrary")),
        name="rglru",
    )(lru_in, *params)


def _gdn_body(qkv_ref, z_ref, ab_ref, cw_ref, alog_ref, dtb_ref, on_ref, y_ref, halo_ref, state_ref):
    @pl.when(pl.program_id(1) == 0)
    def _():
        halo_ref[...] = jnp.zeros_like(halo_ref)
        state_ref[...] = jnp.zeros_like(state_ref)

    x = qkv_ref[...].astype(F32)
    t = x.shape[0]
    c = _causal_conv(x, halo_ref, cw_ref[...])
    c = c * _sigmoid(c)

    ab = ab_ref[...]
    lane = lax.broadcasted_iota(jnp.int32, (t, LANES), 1)
    g = jnp.where(lane < GDN_HEADS, -jnp.exp(alog_ref[...]) * _softplus(ab + dtb_ref[...]), 0.0)
    beta_all = _sigmoid(ab)

    pos = lax.broadcasted_iota(jnp.int32, (t, LANES), 0) % CHUNK
    gc = g
    rc = g
    d = 1
    while d < CHUNK:
        gc = gc + jnp.where(pos >= d, pltpu.roll(gc, d, 0), 0.0)
        rc = rc + jnp.where(pos + d < CHUNK, pltpu.roll(rc, t - d, 0), 0.0)
        d *= 2
    tail = rc - g
    gct = gc.T

    n_chunks = t // CHUNK
    ri = lax.broadcasted_iota(jnp.int32, (t, t), 0)
    ci = lax.broadcasted_iota(jnp.int32, (t, t), 1)
    same = (ri // CHUNK) == (ci // CHUNK)
    causal = jnp.logical_and(same, ri >= ci)
    strict = jnp.logical_and(same, ri > ci)
    eye = jnp.where(ri == ci, 1.0, 0.0)
    col_chunk = lax.broadcasted_iota(jnp.int32, (GDN_DK, t), 1) // CHUNK
    n_levels = int(math.log2(CHUNK))

    heads = range(GDN_HEADS)
    qs, ks, kbs, decays, gccs, rhss = [], [], [], [], [], []
    for h in heads:
        q = c[:, h * GDN_DK:(h + 1) * GDN_DK]
        k = c[:, GDN_WIDTH + h * GDN_DK:GDN_WIDTH + (h + 1) * GDN_DK]
        v = c[:, 2 * GDN_WIDTH + h * GDN_DV:2 * GDN_WIDTH + (h + 1) * GDN_DV]
        q = q * (lax.rsqrt(jnp.sum(q * q, axis=-1, keepdims=True) + RMS_EPS) * (GDN_DK ** -0.5))
        k = k * lax.rsqrt(jnp.sum(k * k, axis=-1, keepdims=True) + RMS_EPS)
        beta = beta_all[:, GDN_HEADS + h:GDN_HEADS + h + 1]
        gcc = gc[:, h:h + 1]
        kb = k * beta
        qs.append(q)
        ks.append(k)
        kbs.append(kb)
        gccs.append(gcc)
        decays.append(jnp.where(causal, jnp.exp(jnp.where(causal, gcc - gct[h:h + 1, :], 0.0)), 0.0))
        rhss.append(jnp.concatenate([v * beta, kb * jnp.exp(gcc)], axis=1).astype(BF16))

    kqs = [_dot_nt(jnp.concatenate([kbs[h], qs[h]], axis=0).astype(BF16), ks[h].astype(BF16)) for h in heads]
    a_lows = [jnp.where(strict, kqs[h][:t] * decays[h], 0.0) for h in heads]
    attns = [kqs[h][t:] * decays[h] for h in heads]

    invs = [eye - a_lows[h] for h in heads]
    pows = []
    for h in heads:
        a16 = a_lows[h].astype(BF16)
        pows.append(_dot(a16, a16))
    for level in range(1, n_levels):
        for h in heads:
            p16 = pows[h].astype(BF16)
            if level < n_levels - 1:
                both = _dot(jnp.concatenate([invs[h].astype(BF16), p16], axis=0), p16)
                invs[h] = invs[h] + both[:t]
                pows[h] = both[t:]
            else:
                invs[h] = invs[h] + _dot(invs[h].astype(BF16), p16)
    uws = [_dot(invs[h].astype(BF16), rhss[h]).astype(BF16) for h in heads]

    prods = []
    for h in heads:
        ke_t = (ks[h] * jnp.exp(tail[:, h:h + 1])).T
        parts = []
        for n in range(n_chunks):
            parts.append(jnp.where(col_chunk == n, ke_t, 0.0))
            parts.append(attns[h][n * CHUNK:(n + 1) * CHUNK, :])
        prods.append(_dot(jnp.concatenate(parts, axis=0).astype(BF16), uws[h]))
    terms = []
    for h in heads:
        prod = prods[h]
        qd = qs[h] * jnp.exp(gccs[h])
        per_chunk = []
        for n in range(n_chunks):
            r0 = n * (GDN_DK + CHUNK)
            rows = slice(n * CHUNK, (n + 1) * CHUNK)
            per_chunk.append(dict(
                lhs=jnp.concatenate([-prod[r0:r0 + GDN_DK, GDN_DV:],
                                     qd[rows] - prod[r0 + GDN_DK:r0 + GDN_DK + CHUNK, GDN_DV:]], axis=0).astype(BF16),
                add_s=prod[r0:r0 + GDN_DK, :GDN_DV],
                add_o=prod[r0 + GDN_DK:r0 + GDN_DK + CHUNK, :GDN_DV],
                dec=jnp.exp(gccs[h][(n + 1) * CHUNK - 1:(n + 1) * CHUNK, :])))
        terms.append(per_chunk)

    states = [state_ref[h] for h in range(GDN_HEADS)]
    outs = [[] for _ in range(GDN_HEADS)]
    rows_h = GDN_DK + CHUNK
    for n in range(n_chunks):
        for h0 in range(0, GDN_HEADS, 2):
            pair = (h0, h0 + 1)
            lhs = jnp.concatenate([terms[h][n]["lhs"] for h in pair], axis=0)
            rhs = jnp.concatenate([states[h] for h in pair], axis=1).astype(BF16)
            both = _dot(lhs, rhs)
            for idx, h in enumerate(pair):
                tm = terms[h][n]
                blk = both[idx * rows_h:(idx + 1) * rows_h, idx * GDN_DV:(idx + 1) * GDN_DV]
                outs[h].append(blk[GDN_DK:] + tm["add_o"])
                states[h] = states[h] * tm["dec"] + blk[:GDN_DK] + tm["add_s"]
    on = on_ref[...]
    for h in range(GDN_HEADS):
        state_ref[h] = states[h]
        o = jnp.concatenate(outs[h], axis=0)
        zz = z_ref[:, h * GDN_DV:(h + 1) * GDN_DV].astype(F32)
        y_ref[:, h * GDN_DV:(h + 1) * GDN_DV] = (_rms(o, on) * (zz * _sigmoid(zz))).astype(y_ref.dtype)


def _gdn(qkv, z, ab, cw, alog, dtb, on):
    b, s, _ = qkv.shape
    t = min(T_GDN, s)
    params = (cw, alog, dtb, on)
    return pl.pallas_call(
        _gdn_body,
        grid=(b, s // t),
        in_specs=[
            pl.BlockSpec((None, t, W_QKV), lambda i, j: (i, j, 0)),
            pl.BlockSpec((None, t, W_Z), lambda i, j: (i, j, 0)),
            pl.BlockSpec((None, t, W_AB), lambda i, j: (i, j, 0)),
        ] + [_const_spec(p.shape) for p in params],
        out_specs=pl.BlockSpec((None, t, GDN_WIDTH), lambda i, j: (i, j, 0)),
        out_shape=jax.ShapeDtypeStruct((b, s, GDN_WIDTH), BF16),
        scratch_shapes=[pltpu.VMEM((SUBLANES, W_QKV), F32), pltpu.VMEM((GDN_HEADS, GDN_DK, GDN_DV), F32)],
        compiler_params=_cparams(("parallel", "arbitrary")),
        name="gdn",
    )(qkv, z, ab, *params)


def _outproj_body(*refs, moe):
    if moe:
        x_ref, om_ref, yl_ref, yg_ref, mon_ref, wo_ref, fn_ref, rw_ref, xo_ref, h_ref, route_ref = refs
    else:
        x_ref, om_ref, yl_ref, yg_ref, mon_ref, wo_ref, fn_ref, xo_ref, h_ref = refs
    ym = _rms(om_ref[...].astype(F32), mon_ref[...]).astype(BF16)
    acc = _dot(ym, wo_ref[:MLA_WIDTH, :])
    acc = acc + _dot(yl_ref[...], wo_ref[MLA_WIDTH:MLA_WIDTH + LRU_WIDTH, :])
    acc = acc + _dot(yg_ref[...], wo_ref[MLA_WIDTH + LRU_WIDTH:, :])
    xn = x_ref[...] + acc
    xo_ref[...] = xn
    h = _rms(xn, fn_ref[...])
    h_ref[...] = h.astype(h_ref.dtype)
    if moe:
        tm = h.shape[0]
        h_hi = h.astype(BF16)
        h_lo = (h - h_hi.astype(F32)).astype(BF16)
        rw = rw_ref[...]
        rw_hi = rw.astype(BF16)
        rw_lo = (rw - rw_hi.astype(F32)).astype(BF16)
        logits = _dot(h_hi, rw_hi) + _dot(h_hi, rw_lo) + _dot(h_lo, rw_hi)
        lane = lax.broadcasted_iota(jnp.int32, (tm, LANES), 1)
        logits = jnp.where(lane < N_EXPERTS, logits, -jnp.inf)
        m1 = jnp.max(logits, axis=-1, keepdims=True)
        i1 = jnp.min(jnp.where(logits == m1, lane, LANES), axis=-1, keepdims=True)
        rest = jnp.where(lane == i1, -jnp.inf, logits)
        m2 = jnp.max(rest, axis=-1, keepdims=True)
        i2 = jnp.min(jnp.where(rest == m2, lane, LANES), axis=-1, keepdims=True)
        e2 = jnp.exp(m2 - m1)
        g1 = 1.0 / (1.0 + e2)
        g2 = e2 * g1
        route = jnp.where(lane == 0, i1.astype(F32),
                          jnp.where(lane == 1, i2.astype(F32),
                                    jnp.where(lane == 2, g1, jnp.where(lane == 3, g2, 0.0))))
        route_ref[...] = route


def _outproj(x, om, yl, yg, mon, wo, fn, rw=None):
    b, s, d = x.shape
    tm = min(TM_OUT, s)
    moe = rw is not None
    row = lambda w: pl.BlockSpec((None, tm, w), lambda i, j: (i, j, 0))
    in_specs = [row(d), row(MLA_WIDTH), row(LRU_WIDTH), row(GDN_WIDTH),
                _const_spec(mon.shape), _const_spec(wo.shape), _const_spec(fn.shape)]
    args = [x, om, yl, yg, mon, wo, fn]
    out_specs = [row(d), row(d)]
    out_shape = [jax.ShapeDtypeStruct((b, s, d), F32), jax.ShapeDtypeStruct((b, s, d), F32 if moe else BF16)]
    if moe:
        in_specs.append(_const_spec(rw.shape))
        args.append(rw)
        out_specs.append(row(LANES))
        out_shape.append(jax.ShapeDtypeStruct((b, s, LANES), F32))
    return pl.pallas_call(
        functools.partial(_outproj_body, moe=moe),
        grid=(b, s // tm),
        in_specs=in_specs,
        out_specs=out_specs,
        out_shape=out_shape,
        compiler_params=_cparams(("parallel", "parallel")),
        name="outproj_moe" if moe else "outproj",
    )(*args)


def _dense_ffn_body(x_ref, h_ref, wg_ref, wu_ref, wd_ref, o_ref, *, fc):
    h = h_ref[...]
    acc = x_ref[...]
    for lo in range(0, wg_ref.shape[1], fc):
        g = _dot(h, wg_ref[:, lo:lo + fc])
        u = _dot(h, wu_ref[:, lo:lo + fc])
        a = (g * _sigmoid(g) * u).astype(BF16)
        acc = acc + _dot(a, wd_ref[lo:lo + fc, :])
    o_ref[...] = acc


def _dense_ffn(x, h, wg, wu, wd):
    n, d = x.shape
    tm = min(TM_FFN, n)
    ff = wg.shape[1]
    fc = FC_DENSE if ff % FC_DENSE == 0 else ff
    resident = lambda shape: pl.BlockSpec(shape, lambda i: (0, 0), pipeline_mode=pl.Buffered(1))
    return pl.pallas_call(
        functools.partial(_dense_ffn_body, fc=fc),
        grid=(n // tm,),
        in_specs=[
            pl.BlockSpec((tm, d), lambda i: (i, 0)),
            pl.BlockSpec((tm, d), lambda i: (i, 0)),
            resident(wg.shape), resident(wu.shape), resident(wd.shape),
        ],
        out_specs=pl.BlockSpec((tm, d), lambda i: (i, 0)),
        out_shape=jax.ShapeDtypeStruct((n, d), F32),
        compiler_params=_cparams(("parallel",)),
        name="dense_ffn",
    )(x, h, wg, wu, wd)


def _row_copy(src_hbm, dst_ref, src_row, dst_row, sem):
    return pltpu.make_async_copy(src_hbm.at[pl.ds(src_row, 1), :], dst_ref.at[pl.ds(dst_row, 1), :], sem)


def _wait_rows(src_hbm, dst_ref, sem):
    pltpu.make_async_copy(src_hbm.at[pl.ds(0, dst_ref.shape[0]), :], dst_ref, sem).wait()


def _moe_body(te_ref, tv_ref, rt_ref, h_hbm, wg_ref, wu_ref, wd_ref, y_ref, xg_ref, xb_ref, acc_ref, sem, *, tm, nf):
    del te_ref
    i = pl.program_id(0)
    f = pl.program_id(1)
    n_tiles = pl.num_programs(0)
    valid = tv_ref[i] > 0
    slot = i % 2
    per_step = tm // nf
    nxt_base = jnp.where(i + 1 < n_tiles, i + 1, 0) * tm + f * per_step

    def prefetch_row(k):
        _row_copy(h_hbm, xg_ref.at[1 - slot], rt_ref[nxt_base + k], f * per_step + k, sem.at[1 - slot]).start()

    @pl.when(jnp.logical_and(i == 0, f == 0))
    def _first_gather():
        def issue(r, carry):
            _row_copy(h_hbm, xg_ref.at[0], rt_ref[r], r, sem.at[0]).start()
            return carry
        lax.fori_loop(0, tm, issue, 0)

    @pl.when(f == 0)
    def _rows_ready():
        _wait_rows(h_hbm, xg_ref.at[slot], sem.at[slot])
        xb_ref[...] = xg_ref[slot].astype(BF16)
        acc_ref[...] = jnp.zeros_like(acc_ref)

    @pl.when(valid)
    def _compute():
        for k in range(per_step):
            prefetch_row(k)
        x = xb_ref[...]
        g = _dot(x, wg_ref[...])
        u = _dot(x, wu_ref[...])
        a = (g * _sigmoid(g) * u).astype(BF16)
        acc_ref[...] += _dot(a, wd_ref[...])

    @pl.when(jnp.logical_not(valid))
    def _prefetch_only():
        def issue(k, carry):
            prefetch_row(k)
            return carry
        lax.fori_loop(0, per_step, issue, 0)

    @pl.when(f == nf - 1)
    def _finish():
        y_ref[...] = acc_ref[...]

    @pl.when(jnp.logical_and(i == n_tiles - 1, f == nf - 1))
    def _drain():
        _wait_rows(h_hbm, xg_ref.at[1 - slot], sem.at[1 - slot])


def _moe_experts(tile_expert, tile_valid, row_token, h, wg, wu, wd, tm):
    n, d = h.shape
    n_tiles = tile_expert.shape[0]
    ff = wg.shape[2]
    tf = TF_MOE if ff % TF_MOE == 0 else ff
    nf = ff // tf
    assert tm % nf == 0 and n_tiles >= 2

    def f_idx(i, f, tv):
        return jnp.where(tv[i] > 0, f, nf - 1)

    grid_spec = pltpu.PrefetchScalarGridSpec(
        num_scalar_prefetch=3,
        grid=(n_tiles, nf),
        in_specs=[
            pl.BlockSpec(memory_space=pl.ANY),
            pl.BlockSpec((None, d, tf), lambda i, f, te, tv, rt: (te[i], 0, f_idx(i, f, tv))),
            pl.BlockSpec((None, d, tf), lambda i, f, te, tv, rt: (te[i], 0, f_idx(i, f, tv))),
            pl.BlockSpec((None, tf, d), lambda i, f, te, tv, rt: (te[i], f_idx(i, f, tv), 0)),
        ],
        out_specs=pl.BlockSpec((tm, d), lambda i, f, te, tv, rt: (i, 0)),
        scratch_shapes=[pltpu.VMEM((2, tm, d), F32), pltpu.VMEM((tm, d), BF16), pltpu.VMEM((tm, d), F32),
                        pltpu.SemaphoreType.DMA((2,))],
    )
    return pl.pallas_call(
        functools.partial(_moe_body, tm=tm, nf=nf),
        grid_spec=grid_spec,
        out_shape=jax.ShapeDtypeStruct((n_tiles * tm, d), F32),
        compiler_params=_cparams(("arbitrary", "arbitrary")),
        name="moe_experts",
    )(tile_expert, tile_valid, row_token, h, wg, wu, wd)


def _combine_body(pos_ref, x_ref, route_ref, y_hbm, o_ref, buf_ref, sem, *, tc):
    i = pl.program_id(0)
    slot = i % 2

    def gather_row(step, dst_slot, r):
        for k in range(TOP_K):
            _row_copy(y_hbm, buf_ref.at[dst_slot, k], pos_ref[(step * tc + r) * TOP_K + k], r,
                      sem.at[dst_slot]).start()

    @pl.when(i == 0)
    def _first_gather():
        def issue(r, carry):
            gather_row(0, 0, r)
            return carry
        lax.fori_loop(0, tc, issue, 0)

    @pl.when(i + 1 < pl.num_programs(0))
    def _prefetch():
        for r in range(tc):
            gather_row(i + 1, 1 - slot, r)

    for k in range(TOP_K):
        _wait_rows(y_hbm, buf_ref.at[slot, k], sem.at[slot])
    acc = x_ref[...]
    route = route_ref[...]
    for k in range(TOP_K):
        acc = acc + route[:, TOP_K + k:TOP_K + k + 1] * buf_ref[slot, k]
    o_ref[...] = acc


def _moe_combine(pos_flat, x, route, y_sorted):
    n, d = x.shape
    tc = min(TC_COMB, n)
    grid_spec = pltpu.PrefetchScalarGridSpec(
        num_scalar_prefetch=1,
        grid=(n // tc,),
        in_specs=[pl.BlockSpec((tc, d), lambda i, pos: (i, 0)), pl.BlockSpec((tc, LANES), lambda i, pos: (i, 0)),
                  pl.BlockSpec(memory_space=pl.ANY)],
        out_specs=pl.BlockSpec((tc, d), lambda i, pos: (i, 0)),
        scratch_shapes=[pltpu.VMEM((2, TOP_K, tc, d), F32), pltpu.SemaphoreType.DMA((2,))],
    )
    return pl.pallas_call(
        functools.partial(_combine_body, tc=tc),
        grid_spec=grid_spec,
        out_shape=jax.ShapeDtypeStruct((n, d), F32),
        compiler_params=_cparams(("arbitrary",)),
        name="moe_combine",
    )(pos_flat, x, route, y_sorted)


def _moe_ffn(x, h, route, wg, wu, wd):
    n, d = x.shape
    tm = min(TM_MOE, n)
    experts = route[:, :TOP_K].astype(jnp.int32).reshape(-1)
    onehot = (experts[:, None] == jnp.arange(N_EXPERTS, dtype=jnp.int32)[None, :]).astype(jnp.int32)
    rank = jnp.sum((jnp.cumsum(onehot, axis=0) - onehot) * onehot, axis=1)
    counts = jnp.sum(onehot, axis=0)
    padded = ((counts + tm - 1) // tm) * tm
    ends = jnp.cumsum(padded)
    starts = ends - padded
    pos = starts[experts] + rank
    n_tiles = (n * TOP_K) // tm + N_EXPERTS
    tile_start = jnp.arange(n_tiles, dtype=jnp.int32) * tm
    tile_valid = (tile_start < ends[-1]).astype(jnp.int32)
    tile_expert = jnp.minimum(jnp.sum((tile_start[:, None] >= ends[None, :]).astype(jnp.int32), axis=1), N_EXPERTS - 1)
    last_expert = jnp.max(jnp.where(counts > 0, jnp.arange(N_EXPERTS, dtype=jnp.int32), 0))
    tile_expert = jnp.where(tile_valid > 0, tile_expert, last_expert).astype(jnp.int32)
    token = jnp.arange(n * TOP_K, dtype=jnp.int32) // TOP_K
    row_token = jnp.zeros((n_tiles * tm,), jnp.int32).at[pos].set(token, unique_indices=True)
    y_sorted = _moe_experts(tile_expert, tile_valid, row_token, h, wg, wu, wd, tm)
    return _moe_combine(pos.astype(jnp.int32), x, route, y_sorted)


def _pad_cols(w, width):
    return jnp.pad(w, ((0, 0), (0, width - w.shape[1])))


def _prep_w_in(w):
    n_mla = MLA_Q_RANK + MLA_KV_RANK + MLA_ROPE
    n_main = n_mla + W_LRU + W_QKV + W_Z
    return jnp.concatenate([_pad_cols(w[:, :n_mla], W_MLA), w[:, n_mla:n_main], _pad_cols(w[:, n_main:], W_AB)],
                           axis=1).astype(BF16)


def _prep_w_uq(w):
    w = w.reshape(MLA_Q_RANK, MLA_HEADS, MLA_QK_DIM)
    w = jnp.pad(w, ((0, 0), (0, 0), (0, QK_PAD - MLA_QK_DIM)))
    return w.reshape(MLA_Q_RANK, MLA_HEADS * QK_PAD).astype(BF16)


def _block_diag(w):
    g, bi, bo = w.shape
    eye = jnp.eye(g, dtype=w.dtype)
    return (eye[:, None, :, None] * w[:, :, None, :]).reshape(g * bi, g * bo).astype(BF16)


def _row(v, width=None):
    v = v.reshape(1, -1).astype(F32)
    return v if width is None else _pad_cols(v, width)


def kernel(x, positions, mix_norm, w_in, mla_q_norm, mla_w_uq, mla_kv_norm, mla_w_ukv, mla_q_head_norm, mla_k_head_norm, mla_out_norm, lru_conv_w, lru_conv_b, lru_w_a, lru_b_a, lru_w_x, lru_b_x, lru_lambda, lru_out_norm, gdn_conv_w, gdn_a_log, gdn_dt_bias, gdn_out_norm, w_out, ffn_norm, dense_w_gate, dense_w_up, dense_w_down, router_w, moe_w_gate, moe_w_up, moe_w_down):
    b, s, d = x.shape
    depth = w_in.shape[0]
    pos3 = positions.reshape(b, s, 1)
    inv_freq = ROPE_THETA ** (-jnp.arange(0, MLA_ROPE, 2, dtype=F32) / MLA_ROPE)
    freq = _row(jnp.concatenate([inv_freq, inv_freq]), LANES)

    for layer in range(depth):
        mla_in, lru_in, qkv_in, z_in, ab_in = _inproj(x, _row(mix_norm[layer]), _prep_w_in(w_in[layer]))

        q, k, v = _mla_prep(
            mla_in, pos3, freq, _row(mla_q_norm[layer]), _prep_w_uq(mla_w_uq[layer]), _row(mla_kv_norm[layer]),
            mla_w_ukv[layer].astype(BF16), _row(mla_q_head_norm[layer], QK_PAD), _row(mla_k_head_norm[layer], QK_PAD))
        o_mla = _attention(q, k, v)

        y_lru = _rglru(
            lru_in, lru_conv_w[layer], _row(lru_conv_b[layer]), _block_diag(lru_w_a[layer]), _row(lru_b_a[layer]),
            _block_diag(lru_w_x[layer]), _row(lru_b_x[layer]), _row(lru_lambda[layer]), _row(lru_out_norm[layer]))

        y_gdn = _gdn(qkv_in, z_in, ab_in, gdn_conv_w[layer], _row(gdn_a_log[layer], LANES),
                     _row(gdn_dt_bias[layer], LANES), _row(gdn_out_norm[layer]))

        wo = w_out[layer].astype(BF16)
        if layer % 2 == 0:
            x, h = _outproj(x, o_mla, y_lru, y_gdn, _row(mla_out_norm[layer]), wo, _row(ffn_norm[layer]))
            e = layer // 2
            x = _dense_ffn(x.reshape(b * s, d), h.reshape(b * s, d), dense_w_gate[e].astype(BF16),
                           dense_w_up[e].astype(BF16), dense_w_down[e].astype(BF16)).reshape(b, s, d)
        else:
            e = layer // 2
            x, h, route = _outproj(x, o_mla, y_lru, y_gdn, _row(mla_out_norm[layer]), wo, _row(ffn_norm[layer]),
                                   _pad_cols(router_w[e].astype(F32), LANES))
            x = _moe_ffn(x.reshape(b * s, d), h.reshape(b * s, d), route.reshape(b * s, LANES),
                         moe_w_gate[e].astype(BF16), moe_w_up[e].astype(BF16),
                         moe_w_down[e].astype(BF16)).reshape(b, s, d)
    return x
```

```python
import functools
import math

import jax
import jax.numpy as jnp
from jax import lax
from jax.experimental import pallas as pl
from jax.experimental.pallas import tpu as pltpu

F32 = jnp.float32
BF16 = jnp.bfloat16
HIGHEST = lax.Precision.HIGHEST

D_MODEL = 1024
CHUNK = 64
RMS_EPS = 1e-6
CONV_WIDTH = 4
MLA_HEADS = 4
MLA_NOPE = 128
MLA_ROPE = 64
MLA_V = 128
MLA_QK_DIM = MLA_NOPE + MLA_ROPE
MLA_Q_RANK = 512
MLA_KV_RANK = 256
MLA_WIDTH = MLA_HEADS * MLA_V
MLA_SCALE = MLA_QK_DIM ** -0.5
ROPE_THETA = 10000.0
LRU_WIDTH = 512
LRU_BLOCKS = 8
LRU_C = 8.0
GDN_HEADS = 4
GDN_DK = 128
GDN_DV = 128
GDN_WIDTH = GDN_HEADS * GDN_DV
N_EXPERTS = 8
TOP_K = 2

LANES = 128
SUBLANES = 8
QK_PAD = 256
V_PAD = 256
LOG2_E = math.log2(math.e)
VMEM_LIMIT = 56 * 1024 * 1024

W_MLA = MLA_Q_RANK + MLA_KV_RANK + LANES
W_LRU = 2 * LRU_WIDTH
W_QKV = 3 * GDN_WIDTH
W_Z = GDN_WIDTH
W_AB = LANES
IN_GROUPS = (W_MLA, W_LRU, W_QKV, W_Z, W_AB)

TM_PROJ = 512
TM_PREP = 512
TQ = 2048
TC_ATTN = 512
TK_MAIN = 2048
TS_LRU = 256
T_GDN = 512
GDN_BLK = 256
TM_OUT = 512
TM_FFN = 512
FC_DENSE = 1408
TM_MOE = 1024
TF_MOE = 1792
FC_MOE = 512
TC_COMB = 256


def _cparams(sem):
    return pltpu.CompilerParams(dimension_semantics=sem, vmem_limit_bytes=VMEM_LIMIT)


def _rms(x, gain):
    return x * lax.rsqrt(jnp.mean(x * x, axis=-1, keepdims=True) + RMS_EPS) * gain


def _sigmoid(x):
    return 1.0 / (1.0 + jnp.exp(-x))


def _softplus(x):
    return jnp.maximum(x, 0.0) + jnp.log1p(jnp.exp(-jnp.abs(x)))


def _dot(a, b, **kw):
    return jnp.dot(a, b, preferred_element_type=F32, **kw)


def _dot_nt(a, b):
    return lax.dot_general(a, b, (((1,), (1,)), ((), ())), preferred_element_type=F32)


def _dot_tn(a, b, **kw):
    return lax.dot_general(a, b, (((0,), (0,)), ((), ())), preferred_element_type=F32, **kw)


def _const_spec(shape):
    nd = len(shape)
    return pl.BlockSpec(shape, lambda *_: (0,) * nd)


def _inproj_body(x_ref, g_ref, w_ref, mla_ref, lru_ref, qkv_ref, z_ref, ab_ref):
    h = _rms(x_ref[...], g_ref[...]).astype(BF16)
    off = 0
    for ref in (mla_ref, lru_ref, qkv_ref, z_ref, ab_ref):
        width = ref.shape[-1]
        ref[...] = _dot(h, w_ref[:, off:off + width]).astype(ref.dtype)
        off += width


def _inproj(x, gain, w_pad):
    b, s, d = x.shape
    tm = min(TM_PROJ, s)
    dtypes = (BF16, BF16, BF16, BF16, F32)
    return pl.pallas_call(
        _inproj_body,
        grid=(b, s // tm),
        in_specs=[
            pl.BlockSpec((None, tm, d), lambda i, j: (i, j, 0)),
            _const_spec((1, d)),
            _const_spec(w_pad.shape),
        ],
        out_specs=[pl.BlockSpec((None, tm, w), lambda i, j: (i, j, 0)) for w in IN_GROUPS],
        out_shape=[jax.ShapeDtypeStruct((b, s, w), dt) for w, dt in zip(IN_GROUPS, dtypes)],
        compiler_params=_cparams(("parallel", "parallel")),
        name="inproj",
    )(x, gain, w_pad)


def _mla_prep_body(mla_ref, pos_ref, freq_ref, qn_ref, wuq_ref, kvn_ref, wukv_ref, qhn_ref, khn_ref,
                   q_ref, k_ref, v_ref):
    m = mla_ref[...].astype(F32)
    tm = m.shape[0]
    cq = m[:, :MLA_Q_RANK]
    ckv = m[:, MLA_Q_RANK:MLA_Q_RANK + MLA_KV_RANK]
    kr = m[:, MLA_Q_RANK + MLA_KV_RANK:]
    qall = _dot(_rms(cq, qn_ref[...]).astype(BF16), wuq_ref[...])
    kvall = _dot(_rms(ckv, kvn_ref[...]).astype(BF16), wukv_ref[...])

    ang = pos_ref[...].astype(F32) * freq_ref[...]
    cos = jnp.cos(ang)
    sin = jnp.sin(ang)
    lane = lax.broadcasted_iota(jnp.int32, (tm, LANES), 1)
    half = MLA_ROPE // 2
    first = lane < half
    sin_signed = jnp.where(first, -sin, sin)

    def rope(x):
        swapped = jnp.where(first, pltpu.roll(x, LANES - half, 1), pltpu.roll(x, half, 1))
        return x * cos + swapped * sin_signed

    qg = qhn_ref[...]
    kg = khn_ref[...]
    kr_ss = jnp.sum(kr * kr, axis=-1, keepdims=True)
    ones_col = jnp.where(lane == 0, 1.0, 0.0).astype(BF16)
    for h in range(MLA_HEADS):
        base = h * QK_PAD
        qn = qall[:, base:base + MLA_NOPE]
        qr = qall[:, base + MLA_NOPE:base + QK_PAD]
        ss = jnp.sum(qn * qn, axis=-1, keepdims=True) + jnp.sum(qr * qr, axis=-1, keepdims=True)
        r = lax.rsqrt(ss * (1.0 / MLA_QK_DIM) + RMS_EPS) * (MLA_SCALE * LOG2_E)
        q_ref[h, :, :MLA_NOPE] = (qn * r * qg[:, :MLA_NOPE]).astype(BF16)
        q_ref[h, :, MLA_NOPE:] = rope(qr * r * qg[:, MLA_NOPE:]).astype(BF16)

        kn = kvall[:, base:base + MLA_NOPE]
        ss = jnp.sum(kn * kn, axis=-1, keepdims=True) + kr_ss
        r = lax.rsqrt(ss * (1.0 / MLA_QK_DIM) + RMS_EPS)
        k_ref[h, :, :MLA_NOPE] = (kn * r * kg[:, :MLA_NOPE]).astype(BF16)
        k_ref[h, :, MLA_NOPE:] = rope(kr * r * kg[:, MLA_NOPE:]).astype(BF16)
        v_ref[h, :, :MLA_V] = kvall[:, base + MLA_NOPE:base + QK_PAD].astype(BF16)
        v_ref[h, :, MLA_V:] = ones_col


def _mla_prep(mla_in, pos3, freq, qn, wuq, kvn, wukv, qhn, khn):
    b, s, _ = mla_in.shape
    tm = min(TM_PREP, s)
    hd = MLA_HEADS
    return pl.pallas_call(
        _mla_prep_body,
        grid=(b, s // tm),
        in_specs=[
            pl.BlockSpec((None, tm, W_MLA), lambda i, j: (i, j, 0)),
            pl.BlockSpec((None, tm, 1), lambda i, j: (i, j, 0)),
            _const_spec(freq.shape), _const_spec(qn.shape), _const_spec(wuq.shape),
            _const_spec(kvn.shape), _const_spec(wukv.shape), _const_spec(qhn.shape), _const_spec(khn.shape),
        ],
        out_specs=[
            pl.BlockSpec((None, hd, tm, QK_PAD), lambda i, j: (i, 0, j, 0)),
            pl.BlockSpec((None, hd, tm, QK_PAD), lambda i, j: (i, 0, j, 0)),
            pl.BlockSpec((None, hd, tm, V_PAD), lambda i, j: (i, 0, j, 0)),
        ],
        out_shape=[
            jax.ShapeDtypeStruct((b, hd, s, QK_PAD), BF16),
            jax.ShapeDtypeStruct((b, hd, s, QK_PAD), BF16),
            jax.ShapeDtypeStruct((b, hd, s, V_PAD), BF16),
        ],
        compiler_params=_cparams(("parallel", "parallel")),
        name="mla_prep",
    )(mla_in, pos3, freq, qn, wuq, kvn, wukv, qhn, khn)


NEG_BIG = -1e30


def _attn_body(q_ref, k_ref, v_ref, o_ref, *, tq, tc, tkm):
    i = pl.program_id(2)
    n_chain = tq // tc
    qs = [q_ref[c * tc:(c + 1) * tc, :] for c in range(n_chain)]

    def steps(chains, starts, sizes, carries, mask):
        scores = [_dot_nt(qs[c], k_ref[pl.ds(st, sz), :]) for c, st, sz in zip(chains, starts, sizes)]
        probs, m_news = [], []
        for c, s in zip(chains, scores):
            if mask is not None:
                s = jnp.where(mask, s, NEG_BIG)
            m_new = jnp.maximum(carries[c][0], jnp.max(s, axis=-1, keepdims=True))
            probs.append(jnp.exp2(s - m_new).astype(BF16))
            m_news.append(m_new)
        out = list(carries)
        for c, st, sz, p, m_new in zip(chains, starts, sizes, probs, m_news):
            m, acc = carries[c]
            out[c] = (m_new, jnp.exp2(m - m_new) * acc + _dot(p, v_ref[pl.ds(st, sz), :]))
        return out

    all_chains = tuple(range(n_chain))

    def full_blocks(j, carries):
        start = pl.multiple_of(j * tkm, tkm)
        return tuple(steps(all_chains, (start,) * n_chain, (tkm,) * n_chain, carries, None))

    init = tuple((jnp.full((tc, 1), NEG_BIG, F32), jnp.zeros((tc, V_PAD), F32)) for _ in range(n_chain))
    carries = list(lax.fori_loop(0, i * (tq // tkm), full_blocks, init))
    tile_start = pl.multiple_of(i * tq, tq)
    if n_chain > 1:
        later = all_chains[1:]
        carries = steps(later, (tile_start,) * len(later), tuple(c * tc for c in later), carries, None)
    diag_mask = (lax.broadcasted_iota(jnp.int32, (tc, tc), 0) // CHUNK
                 >= lax.broadcasted_iota(jnp.int32, (tc, tc), 1) // CHUNK)
    carries = steps(all_chains, tuple(pl.multiple_of(tile_start + c * tc, tc) for c in all_chains),
                    (tc,) * n_chain, carries, diag_mask)
    for c in range(n_chain):
        acc = carries[c][1]
        o_ref[c * tc:(c + 1) * tc, :] = (acc[:, :MLA_V] / acc[:, MLA_V:MLA_V + 1]).astype(o_ref.dtype)


def _attention(q, k, v):
    b, hd, s, _ = q.shape
    tq = min(TQ, s)
    tc = min(TC_ATTN, tq)
    tkm = min(TK_MAIN, tq)
    return pl.pallas_call(
        functools.partial(_attn_body, tq=tq, tc=tc, tkm=tkm),
        grid=(b, hd, s // tq),
        in_specs=[
            pl.BlockSpec((None, None, tq, QK_PAD), lambda bi, h, i: (bi, h, i, 0)),
            pl.BlockSpec((None, None, s, QK_PAD), lambda bi, h, i: (bi, h, 0, 0)),
            pl.BlockSpec((None, None, s, V_PAD), lambda bi, h, i: (bi, h, 0, 0)),
        ],
        out_specs=pl.BlockSpec((None, tq, MLA_V), lambda bi, h, i: (bi, i, h)),
        out_shape=jax.ShapeDtypeStruct((b, s, MLA_WIDTH), BF16),
        compiler_params=_cparams(("parallel", "parallel", "arbitrary")),
        name="mla_attention",
    )(q, k, v)


def _causal_conv(x, halo_ref, w):
    t = x.shape[0]
    xe = jnp.concatenate([halo_ref[...], x], axis=0)
    halo_ref[...] = x[t - SUBLANES:, :]
    out = xe[SUBLANES:, :] * w[CONV_WIDTH - 1:CONV_WIDTH, :]
    for j in range(CONV_WIDTH - 1):
        shift = CONV_WIDTH - 1 - j
        out = out + xe[SUBLANES - shift:SUBLANES - shift + t, :] * w[j:j + 1, :]
    return out


def _lru_body(lru_ref, cw_ref, cb_ref, wa_ref, ba_ref, wx_ref, bx_ref, lam_ref, on_ref, y_ref, halo_ref, h_ref):
    @pl.when(pl.program_id(1) == 0)
    def _():
        halo_ref[...] = jnp.zeros_like(halo_ref)
        h_ref[...] = jnp.zeros_like(h_ref)

    blk = lru_ref[...].astype(F32)
    ts = blk.shape[0]
    x = blk[:, :LRU_WIDTH]
    gate = blk[:, LRU_WIDTH:]
    xc = _causal_conv(x, halo_ref, cw_ref[...]) + cb_ref[...]
    xb = xc.astype(BF16)
    r = _sigmoid(_dot(xb, wa_ref[...]) + ba_ref[...])
    ig = _sigmoid(_dot(xb, wx_ref[...]) + bx_ref[...])
    log_a = (-LRU_C) * r * _softplus(-lam_ref[...])
    a = jnp.exp(log_a)
    u = jnp.sqrt(-jnp.tanh(log_a) * (a * a + 1.0)) * (ig * xc)

    row = lax.broadcasted_iota(jnp.int32, (ts, LRU_WIDTH), 0)
    d = 1
    while d < ts:
        keep = row >= d
        a_prev = jnp.where(keep, pltpu.roll(a, d, 0), 1.0)
        u_prev = jnp.where(keep, pltpu.roll(u, d, 0), 0.0)
        u = a * u_prev + u
        a = a * a_prev
        d *= 2
    h = a * h_ref[...] + u
    h_ref[...] = h[ts - 1:ts, :]
    y = h * jax.nn.gelu(gate, approximate=True)
    y_ref[...] = _rms(y, on_ref[...]).astype(y_ref.dtype)


def _rglru(lru_in, cw, cb, wa, ba, wx, bx, lam, on):
    b, s, _ = lru_in.shape
    ts = min(TS_LRU, s)
    params = (cw, cb, wa, ba, wx, bx, lam, on)
    return pl.pallas_call(
        _lru_body,
        grid=(b, s // ts),
        in_specs=[pl.BlockSpec((None, ts, W_LRU), lambda i, j: (i, j, 0))] + [_const_spec(p.shape) for p in params],
        out_specs=pl.BlockSpec((None, ts, LRU_WIDTH), lambda i, j: (i, j, 0)),
        out_shape=jax.ShapeDtypeStruct((b, s, LRU_WIDTH), BF16),
        scratch_shapes=[pltpu.VMEM((SUBLANES, LRU_WIDTH), F32), pltpu.VMEM((1, LRU_WIDTH), F32)],
        compiler_params=_cparams(("parallel", "arbitrary")),
        name="rglru",
    )(lru_in, *params)


def _gdn_body(qkv_ref, z_ref, ab_ref, cw_ref, alog_ref, dtb_ref, on_ref, y_ref, halo_ref, state_ref):
    @pl.when(pl.program_id(1) == 0)
    def _():
        halo_ref[...] = jnp.zeros_like(halo_ref)
        state_ref[...] = jnp.zeros_like(state_ref)

    x = qkv_ref[...].astype(F32)
    t = x.shape[0]
    c = _causal_conv(x, halo_ref, cw_ref[...])
    c = c * _sigmoid(c)

    ab = ab_ref[...]
    lane = lax.broadcasted_iota(jnp.int32, (t, LANES), 1)
    g = jnp.where(lane < GDN_HEADS, -jnp.exp(alog_ref[...]) * _softplus(ab + dtb_ref[...]), 0.0)
    beta_all = _sigmoid(ab)

    pos = lax.broadcasted_iota(jnp.int32, (t, LANES), 0) % CHUNK
    gc = g
    rc = g
    d = 1
    while d < CHUNK:
        gc = gc + jnp.where(pos >= d, pltpu.roll(gc, d, 0), 0.0)
        rc = rc + jnp.where(pos + d < CHUNK, pltpu.roll(rc, t - d, 0), 0.0)
        d *= 2
    tail = rc - g
    gct = gc.T

    blk = min(GDN_BLK, t)
    n_chunks = t // CHUNK
    per_blk = blk // CHUNK
    ri = lax.broadcasted_iota(jnp.int32, (blk, blk), 0)
    ci = lax.broadcasted_iota(jnp.int32, (blk, blk), 1)
    same = (ri // CHUNK) == (ci // CHUNK)
    causal = jnp.logical_and(same, ri >= ci)
    strict = jnp.logical_and(same, ri > ci)
    eye = jnp.where(ri == ci, 1.0, 0.0)
    col_chunk = lax.broadcasted_iota(jnp.int32, (GDN_DK, blk), 1) // CHUNK
    n_levels = int(math.log2(CHUNK))

    units = [(sb, h) for sb in range(t // blk) for h in range(GDN_HEADS)]
    qs, ks, kbs, decays, gccs, rhss = [], [], [], [], [], []
    for sb, h in units:
        rows = slice(sb * blk, (sb + 1) * blk)
        q = c[rows, h * GDN_DK:(h + 1) * GDN_DK]
        k = c[rows, GDN_WIDTH + h * GDN_DK:GDN_WIDTH + (h + 1) * GDN_DK]
        v = c[rows, 2 * GDN_WIDTH + h * GDN_DV:2 * GDN_WIDTH + (h + 1) * GDN_DV]
        q = q * (lax.rsqrt(jnp.sum(q * q, axis=-1, keepdims=True) + RMS_EPS) * (GDN_DK ** -0.5))
        k = k * lax.rsqrt(jnp.sum(k * k, axis=-1, keepdims=True) + RMS_EPS)
        beta = beta_all[rows, GDN_HEADS + h:GDN_HEADS + h + 1]
        gcc = gc[rows, h:h + 1]
        kb = k * beta
        qs.append(q)
        ks.append(k)
        kbs.append(kb)
        gccs.append(gcc)
        decays.append(jnp.where(causal, jnp.exp(jnp.where(causal, gcc - gct[h:h + 1, rows], 0.0)), 0.0))
        rhss.append(jnp.concatenate([v * beta, kb * jnp.exp(gcc)], axis=1).astype(BF16))

    ids = range(len(units))
    kqs = [_dot_nt(jnp.concatenate([kbs[u], qs[u]], axis=0).astype(BF16), ks[u].astype(BF16)) for u in ids]
    a_lows = [jnp.where(strict, kqs[u][:blk] * decays[u], 0.0) for u in ids]
    attns = [kqs[u][blk:] * decays[u] for u in ids]

    invs = [eye - a_lows[u] for u in ids]
    pows = []
    for u in ids:
        a16 = a_lows[u].astype(BF16)
        pows.append(_dot(a16, a16))
    for level in range(1, n_levels):
        for u in ids:
            p16 = pows[u].astype(BF16)
            if level < n_levels - 1:
                both = _dot(jnp.concatenate([invs[u].astype(BF16), p16], axis=0), p16)
                invs[u] = invs[u] + both[:blk]
                pows[u] = both[blk:]
            else:
                invs[u] = invs[u] + _dot(invs[u].astype(BF16), p16)
    uws = [_dot(invs[u].astype(BF16), rhss[u]).astype(BF16) for u in ids]

    prods = []
    for u, (sb, h) in enumerate(units):
        ke_t = (ks[u] * jnp.exp(tail[sb * blk:(sb + 1) * blk, h:h + 1])).T
        parts = []
        for n in range(per_blk):
            parts.append(jnp.where(col_chunk == n, ke_t, 0.0))
            parts.append(attns[u][n * CHUNK:(n + 1) * CHUNK, :])
        prods.append(_dot(jnp.concatenate(parts, axis=0).astype(BF16), uws[u]))
    terms = [[None] * n_chunks for _ in range(GDN_HEADS)]
    for u, (sb, h) in enumerate(units):
        prod = prods[u]
        qd = qs[u] * jnp.exp(gccs[u])
        for n in range(per_blk):
            r0 = n * (GDN_DK + CHUNK)
            rows = slice(n * CHUNK, (n + 1) * CHUNK)
            terms[h][sb * per_blk + n] = dict(
                lhs=jnp.concatenate([-prod[r0:r0 + GDN_DK, GDN_DV:],
                                     qd[rows] - prod[r0 + GDN_DK:r0 + GDN_DK + CHUNK, GDN_DV:]], axis=0).astype(BF16),
                add_s=prod[r0:r0 + GDN_DK, :GDN_DV],
                add_o=prod[r0 + GDN_DK:r0 + GDN_DK + CHUNK, :GDN_DV],
                dec=jnp.exp(gccs[u][(n + 1) * CHUNK - 1:(n + 1) * CHUNK, :]))

    states = [state_ref[h] for h in range(GDN_HEADS)]
    outs = [[] for _ in range(GDN_HEADS)]
    rows_h = GDN_DK + CHUNK
    for n in range(n_chunks):
        for h0 in range(0, GDN_HEADS, 2):
            pair = (h0, h0 + 1)
            lhs = jnp.concatenate([terms[h][n]["lhs"] for h in pair], axis=0)
            rhs = jnp.concatenate([states[h] for h in pair], axis=1).astype(BF16)
            both = _dot(lhs, rhs)
            for idx, h in enumerate(pair):
                tm = terms[h][n]
                blk = both[idx * rows_h:(idx + 1) * rows_h, idx * GDN_DV:(idx + 1) * GDN_DV]
                outs[h].append(blk[GDN_DK:] + tm["add_o"])
                states[h] = states[h] * tm["dec"] + blk[:GDN_DK] + tm["add_s"]
    on = on_ref[...]
    for h in range(GDN_HEADS):
        state_ref[h] = states[h]
        o = jnp.concatenate(outs[h], axis=0)
        zz = z_ref[:, h * GDN_DV:(h + 1) * GDN_DV].astype(F32)
        y_ref[:, h * GDN_DV:(h + 1) * GDN_DV] = (_rms(o, on) * (zz * _sigmoid(zz))).astype(y_ref.dtype)


def _gdn(qkv, z, ab, cw, alog, dtb, on):
    b, s, _ = qkv.shape
    t = min(T_GDN, s)
    params = (cw, alog, dtb, on)
    return pl.pallas_call(
        _gdn_body,
        grid=(b, s // t),
        in_specs=[
            pl.BlockSpec((None, t, W_QKV), lambda i, j: (i, j, 0)),
            pl.BlockSpec((None, t, W_Z), lambda i, j: (i, j, 0)),
            pl.BlockSpec((None, t, W_AB), lambda i, j: (i, j, 0)),
        ] + [_const_spec(p.shape) for p in params],
        out_specs=pl.BlockSpec((None, t, GDN_WIDTH), lambda i, j: (i, j, 0)),
        out_shape=jax.ShapeDtypeStruct((b, s, GDN_WIDTH), BF16),
        scratch_shapes=[pltpu.VMEM((SUBLANES, W_QKV), F32), pltpu.VMEM((GDN_HEADS, GDN_DK, GDN_DV), F32)],
        compiler_params=_cparams(("parallel", "arbitrary")),
        name="gdn",
    )(qkv, z, ab, *params)


def _outproj_body(*refs, moe):
    if moe:
        x_ref, om_ref, yl_ref, yg_ref, mon_ref, wo_ref, fn_ref, rw_ref, xo_ref, h_ref, route_ref = refs
    else:
        x_ref, om_ref, yl_ref, yg_ref, mon_ref, wo_ref, fn_ref, xo_ref, h_ref = refs
    ym = _rms(om_ref[...].astype(F32), mon_ref[...]).astype(BF16)
    acc = _dot(ym, wo_ref[:MLA_WIDTH, :])
    acc = acc + _dot(yl_ref[...], wo_ref[MLA_WIDTH:MLA_WIDTH + LRU_WIDTH, :])
    acc = acc + _dot(yg_ref[...], wo_ref[MLA_WIDTH + LRU_WIDTH:, :])
    xn = x_ref[...] + acc
    xo_ref[...] = xn
    h = _rms(xn, fn_ref[...])
    h_ref[...] = h.astype(h_ref.dtype)
    if moe:
        tm = h.shape[0]
        h_hi = h.astype(BF16)
        h_lo = (h - h_hi.astype(F32)).astype(BF16)
        rw = rw_ref[...]
        rw_hi = rw.astype(BF16)
        rw_lo = (rw - rw_hi.astype(F32)).astype(BF16)
        logits = _dot(h_hi, rw_hi) + _dot(h_hi, rw_lo) + _dot(h_lo, rw_hi)
        lane = lax.broadcasted_iota(jnp.int32, (tm, LANES), 1)
        logits = jnp.where(lane < N_EXPERTS, logits, -jnp.inf)
        m1 = jnp.max(logits, axis=-1, keepdims=True)
        i1 = jnp.min(jnp.where(logits == m1, lane, LANES), axis=-1, keepdims=True)
        rest = jnp.where(lane == i1, -jnp.inf, logits)
        m2 = jnp.max(rest, axis=-1, keepdims=True)
        i2 = jnp.min(jnp.where(rest == m2, lane, LANES), axis=-1, keepdims=True)
        e2 = jnp.exp(m2 - m1)
        g1 = 1.0 / (1.0 + e2)
        g2 = e2 * g1
        route = jnp.where(lane == 0, i1.astype(F32),
                          jnp.where(lane == 1, i2.astype(F32),
                                    jnp.where(lane == 2, g1, jnp.where(lane == 3, g2, 0.0))))
        route_ref[...] = route


def _outproj(x, om, yl, yg, mon, wo, fn, rw=None):
    b, s, d = x.shape
    tm = min(TM_OUT, s)
    moe = rw is not None
    row = lambda w: pl.BlockSpec((None, tm, w), lambda i, j: (i, j, 0))
    in_specs = [row(d), row(MLA_WIDTH), row(LRU_WIDTH), row(GDN_WIDTH),
                _const_spec(mon.shape), _const_spec(wo.shape), _const_spec(fn.shape)]
    args = [x, om, yl, yg, mon, wo, fn]
    out_specs = [row(d), row(d)]
    out_shape = [jax.ShapeDtypeStruct((b, s, d), F32), jax.ShapeDtypeStruct((b, s, d), F32 if moe else BF16)]
    if moe:
        in_specs.append(_const_spec(rw.shape))
        args.append(rw)
        out_specs.append(row(LANES))
        out_shape.append(jax.ShapeDtypeStruct((b, s, LANES), F32))
    return pl.pallas_call(
        functools.partial(_outproj_body, moe=moe),
        grid=(b, s // tm),
        in_specs=in_specs,
        out_specs=out_specs,
        out_shape=out_shape,
        compiler_params=_cparams(("parallel", "parallel")),
        name="outproj_moe" if moe else "outproj",
    )(*args)


def _dense_ffn_body(x_ref, h_ref, wg_ref, wu_ref, wd_ref, o_ref, *, fc):
    h = h_ref[...]
    acc = x_ref[...]
    for lo in range(0, wg_ref.shape[1], fc):
        g = _dot(h, wg_ref[:, lo:lo + fc])
        u = _dot(h, wu_ref[:, lo:lo + fc])
        a = (g * _sigmoid(g) * u).astype(BF16)
        acc = acc + _dot(a, wd_ref[lo:lo + fc, :])
    o_ref[...] = acc


def _dense_ffn(x, h, wg, wu, wd):
    n, d = x.shape
    tm = min(TM_FFN, n)
    ff = wg.shape[1]
    fc = FC_DENSE if ff % FC_DENSE == 0 else ff
    resident = lambda shape: pl.BlockSpec(shape, lambda i: (0, 0), pipeline_mode=pl.Buffered(1))
    return pl.pallas_call(
        functools.partial(_dense_ffn_body, fc=fc),
        grid=(n // tm,),
        in_specs=[
            pl.BlockSpec((tm, d), lambda i: (i, 0)),
            pl.BlockSpec((tm, d), lambda i: (i, 0)),
            resident(wg.shape), resident(wu.shape), resident(wd.shape),
        ],
        out_specs=pl.BlockSpec((tm, d), lambda i: (i, 0)),
        out_shape=jax.ShapeDtypeStruct((n, d), F32),
        compiler_params=_cparams(("parallel",)),
        name="dense_ffn",
    )(x, h, wg, wu, wd)


def _row_copy(src_hbm, dst_ref, src_row, dst_row, sem):
    return pltpu.make_async_copy(src_hbm.at[pl.ds(src_row, 1), :], dst_ref.at[pl.ds(dst_row, 1), :], sem)


def _wait_rows(src_hbm, dst_ref, sem):
    pltpu.make_async_copy(src_hbm.at[pl.ds(0, dst_ref.shape[0]), :], dst_ref, sem).wait()


def _moe_body(te_ref, tv_ref, rt_ref, h_hbm, wg_ref, wu_ref, wd_ref, y_ref, xg_ref, xb_ref, a_ref, sem, *, tm, nf):
    del te_ref
    i = pl.program_id(0)
    f = pl.program_id(1)
    n_tiles = pl.num_programs(0)
    valid = tv_ref[i] > 0
    slot = i % 2
    per_step = tm // nf
    nxt_base = jnp.where(i + 1 < n_tiles, i + 1, 0) * tm + f * per_step

    def prefetch_row(k, priority=0):
        _row_copy(h_hbm, xg_ref.at[1 - slot], rt_ref[nxt_base + k], f * per_step + k,
                  sem.at[1 - slot]).start(priority=priority)

    @pl.when(jnp.logical_and(i == 0, f == 0))
    def _first_gather():
        def issue(r, carry):
            _row_copy(h_hbm, xg_ref.at[0], rt_ref[r], r, sem.at[0]).start()
            return carry
        lax.fori_loop(0, tm, issue, 0)

    @pl.when(f == 0)
    def _rows_ready():
        _wait_rows(h_hbm, xg_ref.at[slot], sem.at[slot])
        xb_ref[...] = xg_ref[slot].astype(BF16)
        y_ref[...] = jnp.zeros_like(y_ref)

    @pl.when(valid)
    def _compute():
        for k in range(per_step):
            prefetch_row(k, priority=k % 2)
        x = xb_ref[...]
        tf = wg_ref.shape[1]
        for lo in range(0, tf, FC_MOE):
            hi = min(lo + FC_MOE, tf)
            g = _dot(x, wg_ref[:, lo:hi])
            u = _dot(x, wu_ref[:, lo:hi])
            a_ref[:, lo:hi] = (g * _sigmoid(g) * u).astype(BF16)
        y_ref[...] += _dot(a_ref[...], wd_ref[...])

    @pl.when(jnp.logical_not(valid))
    def _prefetch_only():
        def issue(k, carry):
            prefetch_row(k)
            return carry
        lax.fori_loop(0, per_step, issue, 0)

    @pl.when(jnp.logical_and(i == n_tiles - 1, f == nf - 1))
    def _drain():
        _wait_rows(h_hbm, xg_ref.at[1 - slot], sem.at[1 - slot])


def _moe_experts(tile_expert, tile_valid, row_token, h, wg, wu, wd, tm):
    n, d = h.shape
    n_tiles = tile_expert.shape[0]
    ff = wg.shape[2]
    tf = TF_MOE if ff % TF_MOE == 0 else ff
    nf = ff // tf
    assert tm % nf == 0 and n_tiles >= 2

    def f_idx(i, f, tv):
        return jnp.where(tv[i] > 0, f, nf - 1)

    grid_spec = pltpu.PrefetchScalarGridSpec(
        num_scalar_prefetch=3,
        grid=(n_tiles, nf),
        in_specs=[
            pl.BlockSpec(memory_space=pl.ANY),
            pl.BlockSpec((None, d, tf), lambda i, f, te, tv, rt: (te[i], 0, f_idx(i, f, tv))),
            pl.BlockSpec((None, d, tf), lambda i, f, te, tv, rt: (te[i], 0, f_idx(i, f, tv))),
            pl.BlockSpec((None, tf, d), lambda i, f, te, tv, rt: (te[i], f_idx(i, f, tv), 0)),
        ],
        out_specs=pl.BlockSpec((tm, d), lambda i, f, te, tv, rt: (i, 0)),
        scratch_shapes=[pltpu.VMEM((2, tm, d), F32), pltpu.VMEM((tm, d), BF16), pltpu.VMEM((tm, tf), BF16),
                        pltpu.SemaphoreType.DMA((2,))],
    )
    return pl.pallas_call(
        functools.partial(_moe_body, tm=tm, nf=nf),
        grid_spec=grid_spec,
        out_shape=jax.ShapeDtypeStruct((n_tiles * tm, d), F32),
        compiler_params=_cparams(("arbitrary", "arbitrary")),
        name="moe_experts",
    )(tile_expert, tile_valid, row_token, h, wg, wu, wd)


def _combine_body(pos_ref, x_ref, route_ref, y_hbm, o_ref, buf_ref, sem, *, tc):
    i = pl.program_id(0)
    slot = i % 2

    def gather_row(step, dst_slot, r):
        for k in range(TOP_K):
            _row_copy(y_hbm, buf_ref.at[dst_slot, k], pos_ref[(step * tc + r) * TOP_K + k], r,
                      sem.at[dst_slot]).start(priority=k)

    @pl.when(i == 0)
    def _first_gather():
        def issue(r, carry):
            gather_row(0, 0, r)
            return carry
        lax.fori_loop(0, tc, issue, 0)

    @pl.when(i + 1 < pl.num_programs(0))
    def _prefetch():
        for r in range(tc):
            gather_row(i + 1, 1 - slot, r)

    for k in range(TOP_K):
        _wait_rows(y_hbm, buf_ref.at[slot, k], sem.at[slot])
    acc = x_ref[...]
    route = route_ref[...]
    for k in range(TOP_K):
        acc = acc + route[:, TOP_K + k:TOP_K + k + 1] * buf_ref[slot, k]
    o_ref[...] = acc


def _moe_combine(pos_flat, x, route, y_sorted):
    n, d = x.shape
    tc = min(TC_COMB, n)
    grid_spec = pltpu.PrefetchScalarGridSpec(
        num_scalar_prefetch=1,
        grid=(n // tc,),
        in_specs=[pl.BlockSpec((tc, d), lambda i, pos: (i, 0)), pl.BlockSpec((tc, LANES), lambda i, pos: (i, 0)),
                  pl.BlockSpec(memory_space=pl.ANY)],
        out_specs=pl.BlockSpec((tc, d), lambda i, pos: (i, 0)),
        scratch_shapes=[pltpu.VMEM((2, TOP_K, tc, d), F32), pltpu.SemaphoreType.DMA((2,))],
    )
    return pl.pallas_call(
        functools.partial(_combine_body, tc=tc),
        grid_spec=grid_spec,
        out_shape=jax.ShapeDtypeStruct((n, d), F32),
        compiler_params=_cparams(("arbitrary",)),
        name="moe_combine",
    )(pos_flat, x, route, y_sorted)


def _moe_ffn(x, h, route, wg, wu, wd):
    n, d = x.shape
    tm = min(TM_MOE, n)
    experts = route[:, :TOP_K].astype(jnp.int32).reshape(-1)
    onehot = (experts[:, None] == jnp.arange(N_EXPERTS, dtype=jnp.int32)[None, :]).astype(jnp.int32)
    rank = jnp.sum((jnp.cumsum(onehot, axis=0) - onehot) * onehot, axis=1)
    counts = jnp.sum(onehot, axis=0)
    padded = ((counts + tm - 1) // tm) * tm
    ends = jnp.cumsum(padded)
    starts = ends - padded
    pos = starts[experts] + rank
    n_tiles = (n * TOP_K) // tm + N_EXPERTS
    tile_start = jnp.arange(n_tiles, dtype=jnp.int32) * tm
    tile_valid = (tile_start < ends[-1]).astype(jnp.int32)
    tile_expert = jnp.minimum(jnp.sum((tile_start[:, None] >= ends[None, :]).astype(jnp.int32), axis=1), N_EXPERTS - 1)
    last_expert = jnp.max(jnp.where(counts > 0, jnp.arange(N_EXPERTS, dtype=jnp.int32), 0))
    tile_expert = jnp.where(tile_valid > 0, tile_expert, last_expert).astype(jnp.int32)
    token = jnp.arange(n * TOP_K, dtype=jnp.int32) // TOP_K
    row_token = jnp.zeros((n_tiles * tm,), jnp.int32).at[pos].set(token, unique_indices=True)
    y_sorted = _moe_experts(tile_expert, tile_valid, row_token, h, wg, wu, wd, tm)
    return _moe_combine(pos.astype(jnp.int32), x, route, y_sorted)


def _pad_cols(w, width):
    return jnp.pad(w, ((0, 0), (0, width - w.shape[1])))


def _prep_w_in(w):
    n_mla = MLA_Q_RANK + MLA_KV_RANK + MLA_ROPE
    n_main = n_mla + W_LRU + W_QKV + W_Z
    return jnp.concatenate([_pad_cols(w[:, :n_mla], W_MLA), w[:, n_mla:n_main], _pad_cols(w[:, n_main:], W_AB)],
                           axis=1).astype(BF16)


def _prep_w_uq(w):
    w = w.reshape(MLA_Q_RANK, MLA_HEADS, MLA_QK_DIM)
    w = jnp.pad(w, ((0, 0), (0, 0), (0, QK_PAD - MLA_QK_DIM)))
    return w.reshape(MLA_Q_RANK, MLA_HEADS * QK_PAD).astype(BF16)


def _block_diag(w):
    g, bi, bo = w.shape
    eye = jnp.eye(g, dtype=w.dtype)
    return (eye[:, None, :, None] * w[:, :, None, :]).reshape(g * bi, g * bo).astype(BF16)


def _row(v, width=None):
    v = v.reshape(1, -1).astype(F32)
    return v if width is None else _pad_cols(v, width)


def kernel(x, positions, mix_norm, w_in, mla_q_norm, mla_w_uq, mla_kv_norm, mla_w_ukv, mla_q_head_norm, mla_k_head_norm, mla_out_norm, lru_conv_w, lru_conv_b, lru_w_a, lru_b_a, lru_w_x, lru_b_x, lru_lambda, lru_out_norm, gdn_conv_w, gdn_a_log, gdn_dt_bias, gdn_out_norm, w_out, ffn_norm, dense_w_gate, dense_w_up, dense_w_down, router_w, moe_w_gate, moe_w_up, moe_w_down):
    b, s, d = x.shape
    depth = w_in.shape[0]
    pos3 = positions.reshape(b, s, 1)
    inv_freq = ROPE_THETA ** (-jnp.arange(0, MLA_ROPE, 2, dtype=F32) / MLA_ROPE)
    freq = _row(jnp.concatenate([inv_freq, inv_freq]), LANES)

    for layer in range(depth):
        mla_in, lru_in, qkv_in, z_in, ab_in = _inproj(x, _row(mix_norm[layer]), _prep_w_in(w_in[layer]))

        q, k, v = _mla_prep(
            mla_in, pos3, freq, _row(mla_q_norm[layer]), _prep_w_uq(mla_w_uq[layer]), _row(mla_kv_norm[layer]),
            mla_w_ukv[layer].astype(BF16), _row(mla_q_head_norm[layer], QK_PAD), _row(mla_k_head_norm[layer], QK_PAD))
        o_mla = _attention(q, k, v)

        y_lru = _rglru(
            lru_in, lru_conv_w[layer], _row(lru_conv_b[layer]), _block_diag(lru_w_a[layer]), _row(lru_b_a[layer]),
            _block_diag(lru_w_x[layer]), _row(lru_b_x[layer]), _row(lru_lambda[layer]), _row(lru_out_norm[layer]))

        y_gdn = _gdn(qkv_in, z_in, ab_in, gdn_conv_w[layer], _row(gdn_a_log[layer], LANES),
                     _row(gdn_dt_bias[layer], LANES), _row(gdn_out_norm[layer]))

        wo = w_out[layer].astype(BF16)
        if layer % 2 == 0:
            x, h = _outproj(x, o_mla, y_lru, y_gdn, _row(mla_out_norm[layer]), wo, _row(ffn_norm[layer]))
            e = layer // 2
            x = _dense_ffn(x.reshape(b * s, d), h.reshape(b * s, d), dense_w_gate[e].astype(BF16),
                           dense_w_up[e].astype(BF16), dense_w_down[e].astype(BF16)).reshape(b, s, d)
        else:
            e = layer // 2
            x, h, route = _outproj(x, o_mla, y_lru, y_gdn, _row(mla_out_norm[layer]), wo, _row(ffn_norm[layer]),
                                   _pad_cols(router_w[e].astype(F32), LANES))
            x = _moe_ffn(x.reshape(b * s, d), h.reshape(b * s, d), route.reshape(b * s, LANES),
                         moe_w_gate[e].astype(BF16), moe_w_up[e].astype(BF16),
                         moe_w_down[e].astype(BF16)).reshape(b, s, d)
    return x
```

```python
import functools
import math

import jax
import jax.numpy as jnp
from jax import lax
from jax.experimental import pallas as pl
from jax.experimental.pallas import tpu as pltpu

F32 = jnp.float32
BF16 = jnp.bfloat16

D_MODEL = 1024
CHUNK = 64
RMS_EPS = 1e-6
CONV_WIDTH = 4
MLA_HEADS = 4
MLA_NOPE = 128
MLA_ROPE = 64
MLA_V = 128
MLA_QK_DIM = MLA_NOPE + MLA_ROPE
MLA_Q_RANK = 512
MLA_KV_RANK = 256
MLA_WIDTH = MLA_HEADS * MLA_V
MLA_SCALE = MLA_QK_DIM ** -0.5
ROPE_THETA = 10000.0
LRU_WIDTH = 512
LRU_BLOCKS = 8
LRU_C = 8.0
GDN_HEADS = 4
GDN_DK = 128
GDN_DV = 128
GDN_WIDTH = GDN_HEADS * GDN_DV
N_EXPERTS = 8
TOP_K = 2

LANES = 128
SUBLANES = 8
QK_PAD = 256
V_PAD = 256
LOG2_E = math.log2(math.e)
VMEM_LIMIT = 56 * 1024 * 1024

W_MLA = MLA_Q_RANK + MLA_KV_RANK + LANES
W_LRU = 2 * LRU_WIDTH
W_QKV = 3 * GDN_WIDTH
W_Z = GDN_WIDTH
W_AB = LANES
IN_GROUPS = (W_MLA, W_LRU, W_QKV, W_Z, W_AB)

TM_PROJ = 512
TM_PREP = 512
TQ = 2048
TC_ATTN = 512
TK_MAIN = 2048
TS_LRU = 256
T_GDN = 512
GDN_BLK = 256
TM_OUT = 512
TM_FFN = 512
FC_DENSE = 1024
TM_MOE = 512
TF_MOE = 1792
FC_MOE = 1024
TC_DISPATCH = 1024
TC_COMB = 512
GATHER_UNROLL = 16


def _cparams(sem):
    return pltpu.CompilerParams(dimension_semantics=sem, vmem_limit_bytes=VMEM_LIMIT)


def _rms(x, gain):
    return x * lax.rsqrt(jnp.mean(x * x, axis=-1, keepdims=True) + RMS_EPS) * gain


def _sigmoid(x):
    return 1.0 / (1.0 + jnp.exp(-x))


def _softplus(x):
    return jnp.maximum(x, 0.0) + jnp.log1p(jnp.exp(-jnp.abs(x)))


def _dot(a, b, **kw):
    return jnp.dot(a, b, preferred_element_type=F32, **kw)


def _dot_nt(a, b):
    return lax.dot_general(a, b, (((1,), (1,)), ((), ())), preferred_element_type=F32)


def _const_spec(shape):
    nd = len(shape)
    return pl.BlockSpec(shape, lambda *_: (0,) * nd)


def _inproj_body(x_ref, g_ref, w_ref, mla_ref, lru_ref, qkv_ref, z_ref, ab_ref):
    h = _rms(x_ref[...], g_ref[...]).astype(BF16)
    proj = _dot(h, w_ref[...])
    off = 0
    for ref in (mla_ref, lru_ref, qkv_ref, z_ref, ab_ref):
        width = ref.shape[-1]
        ref[...] = proj[:, off:off + width].astype(ref.dtype)
        off += width


def _inproj(x, gain, w_pad):
    b, s, d = x.shape
    tm = min(TM_PROJ, s)
    dtypes = (BF16, BF16, BF16, BF16, F32)
    return pl.pallas_call(
        _inproj_body,
        grid=(b, s // tm),
        in_specs=[
            pl.BlockSpec((None, tm, d), lambda i, j: (i, j, 0)),
            _const_spec((1, d)),
            _const_spec(w_pad.shape),
        ],
        out_specs=[pl.BlockSpec((None, tm, w), lambda i, j: (i, j, 0)) for w in IN_GROUPS],
        out_shape=[jax.ShapeDtypeStruct((b, s, w), dt) for w, dt in zip(IN_GROUPS, dtypes)],
        compiler_params=_cparams(("parallel", "parallel")),
        name="inproj",
    )(x, gain, w_pad)


def _mla_prep_body(mla_ref, pos_ref, freq_ref, qn_ref, wuq_ref, kvn_ref, wukv_ref, qhn_ref, khn_ref,
                   q_ref, k_ref, v_ref):
    m = mla_ref[...].astype(F32)
    tm = m.shape[0]
    cq = m[:, :MLA_Q_RANK]
    ckv = m[:, MLA_Q_RANK:MLA_Q_RANK + MLA_KV_RANK]
    kr = m[:, MLA_Q_RANK + MLA_KV_RANK:]
    qall = _dot(_rms(cq, qn_ref[...]).astype(BF16), wuq_ref[...])
    kvall = _dot(_rms(ckv, kvn_ref[...]).astype(BF16), wukv_ref[...])

    ang = pos_ref[...].astype(F32) * freq_ref[...]
    cos = jnp.cos(ang)
    sin = jnp.sin(ang)
    lane = lax.broadcasted_iota(jnp.int32, (tm, LANES), 1)
    half = MLA_ROPE // 2
    first = lane < half
    sin_signed = jnp.where(first, -sin, sin)

    def rope(x):
        swapped = jnp.where(first, pltpu.roll(x, LANES - half, 1), pltpu.roll(x, half, 1))
        return x * cos + swapped * sin_signed

    qg = qhn_ref[...]
    kg = khn_ref[...]
    kr_ss = jnp.sum(kr * kr, axis=-1, keepdims=True)
    ones_col = jnp.where(lane == 0, 1.0, 0.0).astype(BF16)
    for h in range(MLA_HEADS):
        base = h * QK_PAD
        qn = qall[:, base:base + MLA_NOPE]
        qr = qall[:, base + MLA_NOPE:base + QK_PAD]
        ss = jnp.sum(qn * qn, axis=-1, keepdims=True) + jnp.sum(qr * qr, axis=-1, keepdims=True)
        r = lax.rsqrt(ss * (1.0 / MLA_QK_DIM) + RMS_EPS) * (MLA_SCALE * LOG2_E)
        q_ref[h, :, :MLA_NOPE] = (qn * r * qg[:, :MLA_NOPE]).astype(BF16)
        q_ref[h, :, MLA_NOPE:] = rope(qr * r * qg[:, MLA_NOPE:]).astype(BF16)

        kn = kvall[:, base:base + MLA_NOPE]
        ss = jnp.sum(kn * kn, axis=-1, keepdims=True) + kr_ss
        r = lax.rsqrt(ss * (1.0 / MLA_QK_DIM) + RMS_EPS)
        k_ref[h, :, :MLA_NOPE] = (kn * r * kg[:, :MLA_NOPE]).astype(BF16)
        k_ref[h, :, MLA_NOPE:] = rope(kr * r * kg[:, MLA_NOPE:]).astype(BF16)
        v_ref[h, :, :MLA_V] = kvall[:, base + MLA_NOPE:base + QK_PAD].astype(BF16)
        v_ref[h, :, MLA_V:] = ones_col


def _mla_prep(mla_in, pos3, freq, qn, wuq, kvn, wukv, qhn, khn):
    b, s, _ = mla_in.shape
    tm = min(TM_PREP, s)
    hd = MLA_HEADS
    return pl.pallas_call(
        _mla_prep_body,
        grid=(b, s // tm),
        in_specs=[
            pl.BlockSpec((None, tm, W_MLA), lambda i, j: (i, j, 0)),
            pl.BlockSpec((None, tm, 1), lambda i, j: (i, j, 0)),
            _const_spec(freq.shape), _const_spec(qn.shape), _const_spec(wuq.shape),
            _const_spec(kvn.shape), _const_spec(wukv.shape), _const_spec(qhn.shape), _const_spec(khn.shape),
        ],
        out_specs=[
            pl.BlockSpec((None, hd, tm, QK_PAD), lambda i, j: (i, 0, j, 0)),
            pl.BlockSpec((None, hd, tm, QK_PAD), lambda i, j: (i, 0, j, 0)),
            pl.BlockSpec((None, hd, tm, V_PAD), lambda i, j: (i, 0, j, 0)),
        ],
        out_shape=[
            jax.ShapeDtypeStruct((b, hd, s, QK_PAD), BF16),
            jax.ShapeDtypeStruct((b, hd, s, QK_PAD), BF16),
            jax.ShapeDtypeStruct((b, hd, s, V_PAD), BF16),
        ],
        compiler_params=_cparams(("parallel", "parallel")),
        name="mla_prep",
    )(mla_in, pos3, freq, qn, wuq, kvn, wukv, qhn, khn)


NEG_BIG = -1e30


def _attn_body(q_ref, k_ref, v_ref, o_ref, *, tq, tc, tkm):
    i = pl.program_id(2)
    n_chain = tq // tc
    qs = [q_ref[c * tc:(c + 1) * tc, :] for c in range(n_chain)]

    def steps(chains, starts, sizes, carries, mask):
        scores = [_dot_nt(qs[c], k_ref[pl.ds(st, sz), :]) for c, st, sz in zip(chains, starts, sizes)]
        probs, m_news = [], []
        for c, s in zip(chains, scores):
            if mask is not None:
                s = jnp.where(mask, s, NEG_BIG)
            m_new = jnp.maximum(carries[c][0], jnp.max(s, axis=-1, keepdims=True))
            probs.append(jnp.exp2(s - m_new).astype(BF16))
            m_news.append(m_new)
        out = list(carries)
        for c, st, sz, p, m_new in zip(chains, starts, sizes, probs, m_news):
            m, acc = carries[c]
            out[c] = (m_new, jnp.exp2(m - m_new) * acc + _dot(p, v_ref[pl.ds(st, sz), :]))
        return out

    all_chains = tuple(range(n_chain))

    def full_blocks(j, carries):
        start = pl.multiple_of(j * tkm, tkm)
        return tuple(steps(all_chains, (start,) * n_chain, (tkm,) * n_chain, carries, None))

    init = tuple((jnp.full((tc, 1), NEG_BIG, F32), jnp.zeros((tc, V_PAD), F32)) for _ in range(n_chain))
    carries = list(lax.fori_loop(0, i * (tq // tkm), full_blocks, init))
    tile_start = pl.multiple_of(i * tq, tq)
    if n_chain > 1:
        later = all_chains[1:]
        carries = steps(later, (tile_start,) * len(later), tuple(c * tc for c in later), carries, None)
    diag_mask = (lax.broadcasted_iota(jnp.int32, (tc, tc), 0) // CHUNK
                 >= lax.broadcasted_iota(jnp.int32, (tc, tc), 1) // CHUNK)
    carries = steps(all_chains, tuple(pl.multiple_of(tile_start + c * tc, tc) for c in all_chains),
                    (tc,) * n_chain, carries, diag_mask)
    for c in range(n_chain):
        acc = carries[c][1]
        o_ref[c * tc:(c + 1) * tc, :] = (acc[:, :MLA_V] / acc[:, MLA_V:MLA_V + 1]).astype(o_ref.dtype)


def _attention(q, k, v):
    b, hd, s, _ = q.shape
    tq = min(TQ, s)
    tc = min(TC_ATTN, tq)
    tkm = min(TK_MAIN, tq)
    return pl.pallas_call(
        functools.partial(_attn_body, tq=tq, tc=tc, tkm=tkm),
        grid=(b, hd, s // tq),
        in_specs=[
            pl.BlockSpec((None, None, tq, QK_PAD), lambda bi, h, i: (bi, h, i, 0)),
            pl.BlockSpec((None, None, s, QK_PAD), lambda bi, h, i: (bi, h, 0, 0)),
            pl.BlockSpec((None, None, s, V_PAD), lambda bi, h, i: (bi, h, 0, 0)),
        ],
        out_specs=pl.BlockSpec((None, tq, MLA_V), lambda bi, h, i: (bi, i, h)),
        out_shape=jax.ShapeDtypeStruct((b, s, MLA_WIDTH), BF16),
        compiler_params=_cparams(("parallel", "parallel", "arbitrary")),
        name="mla_attention",
    )(q, k, v)


def _causal_conv(x, halo_ref, w):
    t = x.shape[0]
    xe = jnp.concatenate([halo_ref[...], x], axis=0)
    halo_ref[...] = x[t - SUBLANES:, :]
    out = xe[SUBLANES:, :] * w[CONV_WIDTH - 1:CONV_WIDTH, :]
    for j in range(CONV_WIDTH - 1):
        shift = CONV_WIDTH - 1 - j
        out = out + xe[SUBLANES - shift:SUBLANES - shift + t, :] * w[j:j + 1, :]
    return out


def _lru_body(lru_ref, cw_ref, cb_ref, wa_ref, ba_ref, wx_ref, bx_ref, lam_ref, on_ref, y_ref, halo_ref, h_ref):
    @pl.when(pl.program_id(1) == 0)
    def _():
        halo_ref[...] = jnp.zeros_like(halo_ref)
        h_ref[...] = jnp.zeros_like(h_ref)

    blk = lru_ref[...].astype(F32)
    ts = blk.shape[0]
    x = blk[:, :LRU_WIDTH]
    gate = blk[:, LRU_WIDTH:]
    xc = _causal_conv(x, halo_ref, cw_ref[...]) + cb_ref[...]
    xb = xc.astype(BF16)
    r = _sigmoid(_dot(xb, wa_ref[...]) + ba_ref[...])
    ig = _sigmoid(_dot(xb, wx_ref[...]) + bx_ref[...])
    log_a = (-LRU_C) * r * _softplus(-lam_ref[...])
    a = jnp.exp(log_a)
    u = jnp.sqrt(-jnp.tanh(log_a) * (a * a + 1.0)) * (ig * xc)

    row = lax.broadcasted_iota(jnp.int32, (ts, LRU_WIDTH), 0)
    d = 1
    while d < ts:
        keep = row >= d
        a_prev = jnp.where(keep, pltpu.roll(a, d, 0), 1.0)
        u_prev = jnp.where(keep, pltpu.roll(u, d, 0), 0.0)
        u = a * u_prev + u
        a = a * a_prev
        d *= 2
    h = a * h_ref[...] + u
    h_ref[...] = h[ts - 1:ts, :]
    y = h * jax.nn.gelu(gate, approximate=True)
    y_ref[...] = _rms(y, on_ref[...]).astype(y_ref.dtype)


def _rglru(lru_in, cw, cb, wa, ba, wx, bx, lam, on):
    b, s, _ = lru_in.shape
    ts = min(TS_LRU, s)
    params = (cw, cb, wa, ba, wx, bx, lam, on)
    return pl.pallas_call(
        _lru_body,
        grid=(b, s // ts),
        in_specs=[pl.BlockSpec((None, ts, W_LRU), lambda i, j: (i, j, 0))] + [_const_spec(p.shape) for p in params],
        out_specs=pl.BlockSpec((None, ts, LRU_WIDTH), lambda i, j: (i, j, 0)),
        out_shape=jax.ShapeDtypeStruct((b, s, LRU_WIDTH), BF16),
        scratch_shapes=[pltpu.VMEM((SUBLANES, LRU_WIDTH), F32), pltpu.VMEM((1, LRU_WIDTH), F32)],
        compiler_params=_cparams(("parallel", "arbitrary")),
        name="rglru",
    )(lru_in, *params)


def _gdn_body(qkv_ref, z_ref, ab_ref, cw_ref, alog_ref, dtb_ref, on_ref, y_ref, halo_ref, state_ref):
    @pl.when(pl.program_id(1) == 0)
    def _():
        halo_ref[...] = jnp.zeros_like(halo_ref)
        state_ref[...] = jnp.zeros_like(state_ref)

    x = qkv_ref[...].astype(F32)
    t = x.shape[0]
    c = _causal_conv(x, halo_ref, cw_ref[...])
    c = c * _sigmoid(c)

    ab = ab_ref[...]
    lane = lax.broadcasted_iota(jnp.int32, (t, LANES), 1)
    g = jnp.where(lane < GDN_HEADS, -jnp.exp(alog_ref[...]) * _softplus(ab + dtb_ref[...]), 0.0)
    beta_all = _sigmoid(ab)

    pos = lax.broadcasted_iota(jnp.int32, (t, LANES), 0) % CHUNK
    gc = g
    rc = g
    d = 1
    while d < CHUNK:
        gc = gc + jnp.where(pos >= d, pltpu.roll(gc, d, 0), 0.0)
        rc = rc + jnp.where(pos + d < CHUNK, pltpu.roll(rc, t - d, 0), 0.0)
        d *= 2
    tail = rc - g
    gct = gc.T

    blk = min(GDN_BLK, t)
    n_chunks = t // CHUNK
    per_blk = blk // CHUNK
    ri = lax.broadcasted_iota(jnp.int32, (blk, blk), 0)
    ci = lax.broadcasted_iota(jnp.int32, (blk, blk), 1)
    same = (ri // CHUNK) == (ci // CHUNK)
    causal = jnp.logical_and(same, ri >= ci)
    strict = jnp.logical_and(same, ri > ci)
    eye = jnp.where(ri == ci, 1.0, 0.0)
    col_chunk = lax.broadcasted_iota(jnp.int32, (GDN_DK, blk), 1) // CHUNK
    n_levels = int(math.log2(CHUNK))

    units = [(sb, h) for sb in range(t // blk) for h in range(GDN_HEADS)]
    qs, ks, kbs, decays, gccs, rhss = [], [], [], [], [], []
    for sb, h in units:
        rows = slice(sb * blk, (sb + 1) * blk)
        q = c[rows, h * GDN_DK:(h + 1) * GDN_DK]
        k = c[rows, GDN_WIDTH + h * GDN_DK:GDN_WIDTH + (h + 1) * GDN_DK]
        v = c[rows, 2 * GDN_WIDTH + h * GDN_DV:2 * GDN_WIDTH + (h + 1) * GDN_DV]
        q = q * (lax.rsqrt(jnp.sum(q * q, axis=-1, keepdims=True) + RMS_EPS) * (GDN_DK ** -0.5))
        k = k * lax.rsqrt(jnp.sum(k * k, axis=-1, keepdims=True) + RMS_EPS)
        beta = beta_all[rows, GDN_HEADS + h:GDN_HEADS + h + 1]
        gcc = gc[rows, h:h + 1]
        kb = k * beta
        qs.append(q)
        ks.append(k)
        kbs.append(kb)
        gccs.append(gcc)
        decays.append(jnp.where(causal, jnp.exp(jnp.where(causal, gcc - gct[h:h + 1, rows], 0.0)), 0.0))
        rhss.append(jnp.concatenate([v * beta, kb * jnp.exp(gcc)], axis=1).astype(BF16))

    ids = range(len(units))
    kqs = [_dot_nt(jnp.concatenate([kbs[u], qs[u]], axis=0).astype(BF16), ks[u].astype(BF16)) for u in ids]
    a_lows = [jnp.where(strict, kqs[u][:blk] * decays[u], 0.0) for u in ids]
    attns = [kqs[u][blk:] * decays[u] for u in ids]

    invs = [eye - a_lows[u] for u in ids]
    pows = []
    for u in ids:
        a16 = a_lows[u].astype(BF16)
        pows.append(_dot(a16, a16))
    for level in range(1, n_levels):
        for u in ids:
            p16 = pows[u].astype(BF16)
            if level < n_levels - 1:
                both = _dot(jnp.concatenate([invs[u].astype(BF16), p16], axis=0), p16)
                invs[u] = invs[u] + both[:blk]
                pows[u] = both[blk:]
            else:
                invs[u] = invs[u] + _dot(invs[u].astype(BF16), p16)
    uws = [_dot(invs[u].astype(BF16), rhss[u]).astype(BF16) for u in ids]

    prods = []
    for u, (sb, h) in enumerate(units):
        ke_t = (ks[u] * jnp.exp(tail[sb * blk:(sb + 1) * blk, h:h + 1])).T
        parts = []
        for n in range(per_blk):
            parts.append(jnp.where(col_chunk == n, ke_t, 0.0))
            parts.append(attns[u][n * CHUNK:(n + 1) * CHUNK, :])
        prods.append(_dot(jnp.concatenate(parts, axis=0).astype(BF16), uws[u]))
    terms = [[None] * n_chunks for _ in range(GDN_HEADS)]
    for u, (sb, h) in enumerate(units):
        prod = prods[u]
        qd = qs[u] * jnp.exp(gccs[u])
        for n in range(per_blk):
            r0 = n * (GDN_DK + CHUNK)
            rows = slice(n * CHUNK, (n + 1) * CHUNK)
            terms[h][sb * per_blk + n] = dict(
                lhs=jnp.concatenate([-prod[r0:r0 + GDN_DK, GDN_DV:],
                                     qd[rows] - prod[r0 + GDN_DK:r0 + GDN_DK + CHUNK, GDN_DV:]], axis=0).astype(BF16),
                add_s=prod[r0:r0 + GDN_DK, :GDN_DV],
                add_o=prod[r0 + GDN_DK:r0 + GDN_DK + CHUNK, :GDN_DV],
                dec=jnp.exp(gccs[u][(n + 1) * CHUNK - 1:(n + 1) * CHUNK, :]))

    states = [state_ref[h] for h in range(GDN_HEADS)]
    outs = [[] for _ in range(GDN_HEADS)]
    rows_h = GDN_DK + CHUNK
    for n in range(n_chunks):
        for h0 in range(0, GDN_HEADS, 2):
            pair = (h0, h0 + 1)
            lhs = jnp.concatenate([terms[h][n]["lhs"] for h in pair], axis=0)
            rhs = jnp.concatenate([states[h] for h in pair], axis=1).astype(BF16)
            both = _dot(lhs, rhs)
            for idx, h in enumerate(pair):
                tm = terms[h][n]
                blk = both[idx * rows_h:(idx + 1) * rows_h, idx * GDN_DV:(idx + 1) * GDN_DV]
                outs[h].append(blk[GDN_DK:] + tm["add_o"])
                states[h] = states[h] * tm["dec"] + blk[:GDN_DK] + tm["add_s"]
    on = on_ref[...]
    for h in range(GDN_HEADS):
        state_ref[h] = states[h]
        o = jnp.concatenate(outs[h], axis=0)
        zz = z_ref[:, h * GDN_DV:(h + 1) * GDN_DV].astype(F32)
        y_ref[:, h * GDN_DV:(h + 1) * GDN_DV] = (_rms(o, on) * (zz * _sigmoid(zz))).astype(y_ref.dtype)


def _gdn(qkv, z, ab, cw, alog, dtb, on):
    b, s, _ = qkv.shape
    t = min(T_GDN, s)
    params = (cw, alog, dtb, on)
    return pl.pallas_call(
        _gdn_body,
        grid=(b, s // t),
        in_specs=[
            pl.BlockSpec((None, t, W_QKV), lambda i, j: (i, j, 0)),
            pl.BlockSpec((None, t, W_Z), lambda i, j: (i, j, 0)),
            pl.BlockSpec((None, t, W_AB), lambda i, j: (i, j, 0)),
        ] + [_const_spec(p.shape) for p in params],
        out_specs=pl.BlockSpec((None, t, GDN_WIDTH), lambda i, j: (i, j, 0)),
        out_shape=jax.ShapeDtypeStruct((b, s, GDN_WIDTH), BF16),
        scratch_shapes=[pltpu.VMEM((SUBLANES, W_QKV), F32), pltpu.VMEM((GDN_HEADS, GDN_DK, GDN_DV), F32)],
        compiler_params=_cparams(("parallel", "arbitrary")),
        name="gdn",
    )(qkv, z, ab, *params)


def _outproj_body(*refs, moe):
    if moe:
        x_ref, om_ref, yl_ref, yg_ref, mon_ref, wo_ref, fn_ref, rw_ref, xo_ref, h_ref, route_ref = refs
    else:
        x_ref, om_ref, yl_ref, yg_ref, mon_ref, wo_ref, fn_ref, xo_ref, h_ref = refs
    ym = _rms(om_ref[...].astype(F32), mon_ref[...]).astype(BF16)
    mixed = jnp.concatenate([ym, yl_ref[...], yg_ref[...]], axis=1)
    xn = x_ref[...] + _dot(mixed, wo_ref[...])
    xo_ref[...] = xn
    h = _rms(xn, fn_ref[...])
    h_ref[...] = h.astype(h_ref.dtype)
    if moe:
        tm = h.shape[0]
        h_hi = h.astype(BF16)
        h_lo = (h - h_hi.astype(F32)).astype(BF16)
        rw = rw_ref[...]
        rw_hi = rw.astype(BF16)
        rw_lo = (rw - rw_hi.astype(F32)).astype(BF16)
        hi_part = _dot(h_hi, jnp.concatenate([rw_hi, rw_lo], axis=1))
        logits = hi_part[:, :LANES] + hi_part[:, LANES:] + _dot(h_lo, rw_hi)
        lane = lax.broadcasted_iota(jnp.int32, (tm, LANES), 1)
        logits = jnp.where(lane < N_EXPERTS, logits, -jnp.inf)
        m1 = jnp.max(logits, axis=-1, keepdims=True)
        i1 = jnp.min(jnp.where(logits == m1, lane, LANES), axis=-1, keepdims=True)
        rest = jnp.where(lane == i1, -jnp.inf, logits)
        m2 = jnp.max(rest, axis=-1, keepdims=True)
        i2 = jnp.min(jnp.where(rest == m2, lane, LANES), axis=-1, keepdims=True)
        e2 = jnp.exp(m2 - m1)
        g1 = 1.0 / (1.0 + e2)
        g2 = e2 * g1
        route = jnp.where(lane == 0, i1.astype(F32),
                          jnp.where(lane == 1, i2.astype(F32),
                                    jnp.where(lane == 2, g1, jnp.where(lane == 3, g2, 0.0))))
        route_ref[...] = route


def _outproj(x, om, yl, yg, mon, wo, fn, rw=None):
    b, s, d = x.shape
    tm = min(TM_OUT, s)
    moe = rw is not None
    row = lambda w: pl.BlockSpec((None, tm, w), lambda i, j: (i, j, 0))
    in_specs = [row(d), row(MLA_WIDTH), row(LRU_WIDTH), row(GDN_WIDTH),
                _const_spec(mon.shape), _const_spec(wo.shape), _const_spec(fn.shape)]
    args = [x, om, yl, yg, mon, wo, fn]
    out_specs = [row(d), row(d)]
    out_shape = [jax.ShapeDtypeStruct((b, s, d), F32), jax.ShapeDtypeStruct((b, s, d), F32 if moe else BF16)]
    if moe:
        in_specs.append(_const_spec(rw.shape))
        args.append(rw)
        out_specs.append(row(LANES))
        out_shape.append(jax.ShapeDtypeStruct((b, s, LANES), F32))
    return pl.pallas_call(
        functools.partial(_outproj_body, moe=moe),
        grid=(b, s // tm),
        in_specs=in_specs,
        out_specs=out_specs,
        out_shape=out_shape,
        compiler_params=_cparams(("parallel", "parallel")),
        name="outproj_moe" if moe else "outproj",
    )(*args)


def _dense_ffn_body(x_ref, h_ref, wg_ref, wu_ref, wd_ref, o_ref, a_ref):
    h = h_ref[...]
    ff = wg_ref.shape[1]
    for lo in range(0, ff, FC_DENSE):
        hi = min(lo + FC_DENSE, ff)
        g = _dot(h, wg_ref[:, lo:hi])
        u = _dot(h, wu_ref[:, lo:hi])
        a_ref[:, lo:hi] = (g * _sigmoid(g) * u).astype(BF16)
    o_ref[...] = x_ref[...] + _dot(a_ref[...], wd_ref[...])


def _dense_ffn(x, h, wg, wu, wd):
    n, d = x.shape
    tm = min(TM_FFN, n)
    ff = wg.shape[1]
    resident = lambda shape: pl.BlockSpec(shape, lambda i: (0, 0), pipeline_mode=pl.Buffered(1))
    return pl.pallas_call(
        _dense_ffn_body,
        grid=(n // tm,),
        in_specs=[
            pl.BlockSpec((tm, d), lambda i: (i, 0)),
            pl.BlockSpec((tm, d), lambda i: (i, 0)),
            resident(wg.shape), resident(wu.shape), resident(wd.shape),
        ],
        out_specs=pl.BlockSpec((tm, d), lambda i: (i, 0)),
        out_shape=jax.ShapeDtypeStruct((n, d), F32),
        scratch_shapes=[pltpu.VMEM((tm, ff), BF16)],
        compiler_params=_cparams(("parallel",)),
        name="dense_ffn",
    )(x, h, wg, wu, wd)


def _row_copy(src_hbm, dst_ref, src_row, dst_row, sem):
    return pltpu.make_async_copy(src_hbm.at[pl.ds(src_row, 1), :], dst_ref.at[pl.ds(dst_row, 1), :], sem)


def _wait_rows(src_hbm, dst_ref, sem):
    pltpu.make_async_copy(src_hbm.at[pl.ds(0, dst_ref.shape[0]), :], dst_ref, sem).wait()


def _dispatch_body(pos_ref, zt_ref, h_hbm, xs_hbm, buf_ref, zero_ref, load_sem, row_sem, zero_sem, *, tc, tm):
    i = pl.program_id(0)
    n_steps = pl.num_programs(0)
    slot = i % 2

    @pl.when(i == 0)
    def _zero_padding_tiles():
        zero_ref[...] = jnp.zeros_like(zero_ref)
        for phase in ("start", "wait"):
            for j in range(zt_ref.shape[0]):
                tile = zt_ref[j]

                @pl.when(tile >= 0)
                def _():
                    dst = xs_hbm.at[pl.ds(pl.multiple_of(jnp.maximum(tile, 0) * tm, tm), tm), :]
                    copy = pltpu.make_async_copy(zero_ref, dst, zero_sem)
                    copy.start() if phase == "start" else copy.wait()

    def block_load(step, dst_slot):
        return pltpu.make_async_copy(h_hbm.at[pl.ds(pl.multiple_of(step * tc, tc), tc), :], buf_ref.at[dst_slot],
                                     load_sem.at[dst_slot])

    def wait_rows_out(src_slot):
        for _ in range(TOP_K):
            pltpu.make_async_copy(buf_ref.at[src_slot], xs_hbm.at[pl.ds(0, tc), :], row_sem.at[src_slot]).wait()

    @pl.when(i == 0)
    def _first_load():
        block_load(0, 0).start()

    block_load(i, slot).wait()
    base = i * tc
    for r in range(tc):
        for k in range(TOP_K):
            pltpu.make_async_copy(buf_ref.at[slot, pl.ds(r, 1), :],
                                  xs_hbm.at[pl.ds(pos_ref[(base + r) * TOP_K + k], 1), :],
                                  row_sem.at[slot]).start(priority=k)

    @pl.when(i > 0)
    def _other_slot_free():
        wait_rows_out(1 - slot)

    @pl.when(i + 1 < n_steps)
    def _next_load():
        block_load(i + 1, 1 - slot).start()

    @pl.when(i == n_steps - 1)
    def _drain():
        wait_rows_out(slot)


def _moe_dispatch(pos_flat, zero_tiles, h, n_rows, tm):
    n, d = h.shape
    tc = min(TC_DISPATCH, n)
    grid_spec = pltpu.PrefetchScalarGridSpec(
        num_scalar_prefetch=2,
        grid=(n // tc,),
        in_specs=[pl.BlockSpec(memory_space=pl.ANY)],
        out_specs=pl.BlockSpec(memory_space=pl.ANY),
        scratch_shapes=[pltpu.VMEM((2, tc, d), F32), pltpu.VMEM((tm, d), F32), pltpu.SemaphoreType.DMA((2,)),
                        pltpu.SemaphoreType.DMA((2,)), pltpu.SemaphoreType.DMA(())],
    )
    return pl.pallas_call(
        functools.partial(_dispatch_body, tc=tc, tm=tm),
        grid_spec=grid_spec,
        out_shape=jax.ShapeDtypeStruct((n_rows, d), F32),
        compiler_params=_cparams(("arbitrary",)),
        name="moe_dispatch",
    )(pos_flat, zero_tiles, h)


def _moe_body(te_ref, tv_ref, x_ref, wg_ref, wu_ref, wd_ref, y_ref, a_ref):
    del te_ref
    i = pl.program_id(0)
    f = pl.program_id(1)

    @pl.when(f == 0)
    def _zero():
        y_ref[...] = jnp.zeros_like(y_ref)

    @pl.when(tv_ref[i] > 0)
    def _compute():
        x = x_ref[...].astype(BF16)
        tf = wg_ref.shape[1]
        for lo in range(0, tf, FC_MOE):
            hi = min(lo + FC_MOE, tf)
            g = _dot(x, wg_ref[:, lo:hi])
            u = _dot(x, wu_ref[:, lo:hi])
            a_ref[:, lo:hi] = (g * _sigmoid(g) * u).astype(BF16)
        y_ref[...] += _dot(a_ref[...], wd_ref[...])


def _moe_experts(tile_expert, tile_valid, xs, wg, wu, wd, tm):
    n_rows, d = xs.shape
    n_tiles = tile_expert.shape[0]
    ff = wg.shape[2]
    tf = TF_MOE if ff % TF_MOE == 0 else ff
    nf = ff // tf

    def f_idx(i, f, tv):
        return jnp.where(tv[i] > 0, f, nf - 1)

    grid_spec = pltpu.PrefetchScalarGridSpec(
        num_scalar_prefetch=2,
        grid=(n_tiles, nf),
        in_specs=[
            pl.BlockSpec((tm, d), lambda i, f, te, tv: (i, 0)),
            pl.BlockSpec((None, d, tf), lambda i, f, te, tv: (te[i], 0, f_idx(i, f, tv))),
            pl.BlockSpec((None, d, tf), lambda i, f, te, tv: (te[i], 0, f_idx(i, f, tv))),
            pl.BlockSpec((None, tf, d), lambda i, f, te, tv: (te[i], f_idx(i, f, tv), 0)),
        ],
        out_specs=pl.BlockSpec((tm, d), lambda i, f, te, tv: (i, 0)),
        scratch_shapes=[pltpu.VMEM((tm, tf), BF16)],
    )
    return pl.pallas_call(
        _moe_body,
        grid_spec=grid_spec,
        out_shape=jax.ShapeDtypeStruct((n_rows, d), F32),
        compiler_params=_cparams(("arbitrary", "arbitrary")),
        name="moe_experts",
    )(tile_expert, tile_valid, xs, wg, wu, wd)


def _combine_body(pos_ref, x_ref, route_ref, y_hbm, o_ref, buf_ref, sem, *, tc):
    i = pl.program_id(0)
    slot = i % 2

    def gather_row(step, dst_slot, r):
        for k in range(TOP_K):
            _row_copy(y_hbm, buf_ref.at[dst_slot, k], pos_ref[(step * tc + r) * TOP_K + k], r,
                      sem.at[dst_slot]).start(priority=k)

    @pl.when(i == 0)
    def _first_gather():
        def issue(r, carry):
            gather_row(0, 0, r)
            return carry
        lax.fori_loop(0, tc, issue, 0, unroll=GATHER_UNROLL)

    @pl.when(i + 1 < pl.num_programs(0))
    def _prefetch():
        for r in range(tc):
            gather_row(i + 1, 1 - slot, r)

    for k in range(TOP_K):
        _wait_rows(y_hbm, buf_ref.at[slot, k], sem.at[slot])
    acc = x_ref[...]
    route = route_ref[...]
    for k in range(TOP_K):
        acc = acc + route[:, TOP_K + k:TOP_K + k + 1] * buf_ref[slot, k]
    o_ref[...] = acc


def _moe_combine(pos_flat, x, route, y_sorted):
    n, d = x.shape
    tc = min(TC_COMB, n)
    grid_spec = pltpu.PrefetchScalarGridSpec(
        num_scalar_prefetch=1,
        grid=(n // tc,),
        in_specs=[pl.BlockSpec((tc, d), lambda i, pos: (i, 0)), pl.BlockSpec((tc, LANES), lambda i, pos: (i, 0)),
                  pl.BlockSpec(memory_space=pl.ANY)],
        out_specs=pl.BlockSpec((tc, d), lambda i, pos: (i, 0)),
        scratch_shapes=[pltpu.VMEM((2, TOP_K, tc, d), F32), pltpu.SemaphoreType.DMA((2,))],
    )
    return pl.pallas_call(
        functools.partial(_combine_body, tc=tc),
        grid_spec=grid_spec,
        out_shape=jax.ShapeDtypeStruct((n, d), F32),
        compiler_params=_cparams(("arbitrary",)),
        name="moe_combine",
    )(pos_flat, x, route, y_sorted)


def _moe_ffn(x, h, route, wg, wu, wd):
    n, d = x.shape
    tm = min(TM_MOE, n)
    experts = route[:, :TOP_K].astype(jnp.int32).reshape(-1)
    onehot = (experts[:, None] == jnp.arange(N_EXPERTS, dtype=jnp.int32)[None, :]).astype(jnp.int32)
    rank = jnp.sum((jnp.cumsum(onehot, axis=0) - onehot) * onehot, axis=1)
    counts = jnp.sum(onehot, axis=0)
    padded = ((counts + tm - 1) // tm) * tm
    ends = jnp.cumsum(padded)
    starts = ends - padded
    pos = starts[experts] + rank
    n_tiles = (n * TOP_K) // tm + N_EXPERTS
    tile_start = jnp.arange(n_tiles, dtype=jnp.int32) * tm
    tile_valid = (tile_start < ends[-1]).astype(jnp.int32)
    tile_expert = jnp.minimum(jnp.sum((tile_start[:, None] >= ends[None, :]).astype(jnp.int32), axis=1), N_EXPERTS - 1)
    last_expert = jnp.max(jnp.where(counts > 0, jnp.arange(N_EXPERTS, dtype=jnp.int32), 0))
    tile_expert = jnp.where(tile_valid > 0, tile_expert, last_expert).astype(jnp.int32)
    pos = pos.astype(jnp.int32)
    group_last = jnp.where(counts > 0, ends // tm - 1, -1)
    tail = ends[-1] // tm + jnp.arange(N_EXPERTS, dtype=jnp.int32)
    tail = jnp.where(tail < n_tiles, tail, -1)
    zero_tiles = jnp.concatenate([group_last, tail]).astype(jnp.int32)
    xs = _moe_dispatch(pos, zero_tiles, h, n_tiles * tm, tm)
    y_sorted = _moe_experts(tile_expert, tile_valid, xs, wg, wu, wd, tm)
    return _moe_combine(pos, x, route, y_sorted)


def _pad_cols(w, width):
    return jnp.pad(w, ((0, 0), (0, width - w.shape[1])))


def _prep_w_in(w):
    n_mla = MLA_Q_RANK + MLA_KV_RANK + MLA_ROPE
    n_main = n_mla + W_LRU + W_QKV + W_Z
    return jnp.concatenate([_pad_cols(w[:, :n_mla], W_MLA), w[:, n_mla:n_main], _pad_cols(w[:, n_main:], W_AB)],
                           axis=1).astype(BF16)


def _prep_w_uq(w):
    w = w.reshape(MLA_Q_RANK, MLA_HEADS, MLA_QK_DIM)
    w = jnp.pad(w, ((0, 0), (0, 0), (0, QK_PAD - MLA_QK_DIM)))
    return w.reshape(MLA_Q_RANK, MLA_HEADS * QK_PAD).astype(BF16)


def _block_diag(w):
    g, bi, bo = w.shape
    eye = jnp.eye(g, dtype=w.dtype)
    return (eye[:, None, :, None] * w[:, :, None, :]).reshape(g * bi, g * bo).astype(BF16)


def _row(v, width=None):
    v = v.reshape(1, -1).astype(F32)
    return v if width is None else _pad_cols(v, width)


def kernel(x, positions, mix_norm, w_in, mla_q_norm, mla_w_uq, mla_kv_norm, mla_w_ukv, mla_q_head_norm, mla_k_head_norm, mla_out_norm, lru_conv_w, lru_conv_b, lru_w_a, lru_b_a, lru_w_x, lru_b_x, lru_lambda, lru_out_norm, gdn_conv_w, gdn_a_log, gdn_dt_bias, gdn_out_norm, w_out, ffn_norm, dense_w_gate, dense_w_up, dense_w_down, router_w, moe_w_gate, moe_w_up, moe_w_down):
    b, s, d = x.shape
    depth = w_in.shape[0]
    pos3 = positions.reshape(b, s, 1)
    inv_freq = ROPE_THETA ** (-jnp.arange(0, MLA_ROPE, 2, dtype=F32) / MLA_ROPE)
    freq = _row(jnp.concatenate([inv_freq, inv_freq]), LANES)

    for layer in range(depth):
        mla_in, lru_in, qkv_in, z_in, ab_in = _inproj(x, _row(mix_norm[layer]), _prep_w_in(w_in[layer]))

        q, k, v = _mla_prep(
            mla_in, pos3, freq, _row(mla_q_norm[layer]), _prep_w_uq(mla_w_uq[layer]), _row(mla_kv_norm[layer]),
            mla_w_ukv[layer].astype(BF16), _row(mla_q_head_norm[layer], QK_PAD), _row(mla_k_head_norm[layer], QK_PAD))
        o_mla = _attention(q, k, v)

        y_lru = _rglru(
            lru_in, lru_conv_w[layer], _row(lru_conv_b[layer]), _block_diag(lru_w_a[layer]), _row(lru_b_a[layer]),
            _block_diag(lru_w_x[layer]), _row(lru_b_x[layer]), _row(lru_lambda[layer]), _row(lru_out_norm[layer]))

        y_gdn = _gdn(qkv_in, z_in, ab_in, gdn_conv_w[layer], _row(gdn_a_log[layer], LANES),
                     _row(gdn_dt_bias[layer], LANES), _row(gdn_out_norm[layer]))

        wo = w_out[layer].astype(BF16)
        if layer % 2 == 0:
            x, h = _outproj(x, o_mla, y_lru, y_gdn, _row(mla_out_norm[layer]), wo, _row(ffn_norm[layer]))
            e = layer // 2
            x = _dense_ffn(x.reshape(b * s, d), h.reshape(b * s, d), dense_w_gate[e].astype(BF16),
                           dense_w_up[e].astype(BF16), dense_w_down[e].astype(BF16)).reshape(b, s, d)
        else:
            e = layer // 2
            x, h, route = _outproj(x, o_mla, y_lru, y_gdn, _row(mla_out_norm[layer]), wo, _row(ffn_norm[layer]),
                                   _pad_cols(router_w[e].astype(F32), LANES))
            x = _moe_ffn(x.reshape(b * s, d), h.reshape(b * s, d), route.reshape(b * s, LANES),
                         moe_w_gate[e].astype(BF16), moe_w_up[e].astype(BF16),
                         moe_w_down[e].astype(BF16)).reshape(b, s, d)
    return x
```

```python
import functools
import math

import jax
import jax.numpy as jnp
from jax import lax
from jax.experimental import pallas as pl
from jax.experimental.pallas import tpu as pltpu

F32 = jnp.float32
BF16 = jnp.bfloat16

D_MODEL = 1024
CHUNK = 64
RMS_EPS = 1e-6
CONV_WIDTH = 4
MLA_HEADS = 4
MLA_NOPE = 128
MLA_ROPE = 64
MLA_V = 128
MLA_QK_DIM = MLA_NOPE + MLA_ROPE
MLA_Q_RANK = 512
MLA_KV_RANK = 256
MLA_WIDTH = MLA_HEADS * MLA_V
MLA_SCALE = MLA_QK_DIM ** -0.5
ROPE_THETA = 10000.0
LRU_WIDTH = 512
LRU_BLOCKS = 8
LRU_C = 8.0
GDN_HEADS = 4
GDN_DK = 128
GDN_DV = 128
GDN_WIDTH = GDN_HEADS * GDN_DV
N_EXPERTS = 8
TOP_K = 2

LANES = 128
SUBLANES = 8
QK_PAD = 256
V_PAD = 256
LOG2_E = math.log2(math.e)
VMEM_LIMIT = 56 * 1024 * 1024

W_MLA = MLA_Q_RANK + MLA_KV_RANK + LANES
W_LRU = 2 * LRU_WIDTH
W_QKV = 3 * GDN_WIDTH
W_Z = GDN_WIDTH
W_AB = LANES
IN_GROUPS = (W_MLA, W_LRU, W_QKV, W_Z, W_AB)

TM_PROJ = 512
TM_PREP = 1024
TQ = 2048
TC_ATTN = 512
TK_MAIN = 2048
TS_LRU = 256
T_GDN = 512
GDN_BLK = 256
TM_OUT = 1024
TM_FFN = 1024
FC_DENSE = 1024
TM_MOE = 512
TF_MOE = 1792
FC_MOE = 1024
TC_DISPATCH = 1024
TC_COMB = 512
GATHER_UNROLL = 16


def _cparams(sem):
    return pltpu.CompilerParams(dimension_semantics=sem, vmem_limit_bytes=VMEM_LIMIT)


def _rms(x, gain):
    return x * lax.rsqrt(jnp.mean(x * x, axis=-1, keepdims=True) + RMS_EPS) * gain


def _sigmoid(x):
    return 1.0 / (1.0 + jnp.exp(-x))


def _softplus(x):
    return jnp.maximum(x, 0.0) + jnp.log1p(jnp.exp(-jnp.abs(x)))


def _dot(a, b, **kw):
    return jnp.dot(a, b, preferred_element_type=F32, **kw)


def _dot_nt(a, b):
    return lax.dot_general(a, b, (((1,), (1,)), ((), ())), preferred_element_type=F32)


def _const_spec(shape):
    nd = len(shape)
    return pl.BlockSpec(shape, lambda *_: (0,) * nd)


def _inproj_body(x_ref, g_ref, w_ref, mla_ref, lru_ref, qkv_ref, z_ref, ab_ref):
    h = _rms(x_ref[...], g_ref[...]).astype(BF16)
    proj = _dot(h, w_ref[...])
    off = 0
    for ref in (mla_ref, lru_ref, qkv_ref, z_ref, ab_ref):
        width = ref.shape[-1]
        ref[...] = proj[:, off:off + width].astype(ref.dtype)
        off += width


def _inproj(x, gain, w_pad):
    b, s, d = x.shape
    tm = min(TM_PROJ, s)
    dtypes = (BF16, BF16, BF16, BF16, F32)
    return pl.pallas_call(
        _inproj_body,
        grid=(b, s // tm),
        in_specs=[
            pl.BlockSpec((None, tm, d), lambda i, j: (i, j, 0)),
            _const_spec((1, d)),
            _const_spec(w_pad.shape),
        ],
        out_specs=[pl.BlockSpec((None, tm, w), lambda i, j: (i, j, 0)) for w in IN_GROUPS],
        out_shape=[jax.ShapeDtypeStruct((b, s, w), dt) for w, dt in zip(IN_GROUPS, dtypes)],
        compiler_params=_cparams(("parallel", "parallel")),
        name="inproj",
    )(x, gain, w_pad)


def _mla_prep_body(mla_ref, pos_ref, freq_ref, qn_ref, wuq_ref, kvn_ref, wukv_ref, qhn_ref, khn_ref,
                   q_ref, k_ref, v_ref):
    m = mla_ref[...].astype(F32)
    tm = m.shape[0]
    cq = m[:, :MLA_Q_RANK]
    ckv = m[:, MLA_Q_RANK:MLA_Q_RANK + MLA_KV_RANK]
    kr = m[:, MLA_Q_RANK + MLA_KV_RANK:]
    qall = _dot(_rms(cq, qn_ref[...]).astype(BF16), wuq_ref[...])
    kvall = _dot(_rms(ckv, kvn_ref[...]).astype(BF16), wukv_ref[...])

    ang = pos_ref[...].astype(F32) * freq_ref[...]
    cos = jnp.cos(ang)
    sin = jnp.sin(ang)
    lane = lax.broadcasted_iota(jnp.int32, (tm, LANES), 1)
    half = MLA_ROPE // 2
    first = lane < half
    sin_signed = jnp.where(first, -sin, sin)

    def rope(x):
        swapped = jnp.where(first, pltpu.roll(x, LANES - half, 1), pltpu.roll(x, half, 1))
        return x * cos + swapped * sin_signed

    qg = qhn_ref[...]
    kg = khn_ref[...]
    kr_ss = jnp.sum(kr * kr, axis=-1, keepdims=True)
    ones_col = jnp.where(lane == 0, 1.0, 0.0).astype(BF16)
    for h in range(MLA_HEADS):
        base = h * QK_PAD
        qn = qall[:, base:base + MLA_NOPE]
        qr = qall[:, base + MLA_NOPE:base + QK_PAD]
        ss = jnp.sum(qn * qn, axis=-1, keepdims=True) + jnp.sum(qr * qr, axis=-1, keepdims=True)
        r = lax.rsqrt(ss * (1.0 / MLA_QK_DIM) + RMS_EPS) * (MLA_SCALE * LOG2_E)
        q_ref[h, :, :MLA_NOPE] = (qn * r * qg[:, :MLA_NOPE]).astype(BF16)
        q_ref[h, :, MLA_NOPE:] = rope(qr * r * qg[:, MLA_NOPE:]).astype(BF16)

        kn = kvall[:, base:base + MLA_NOPE]
        ss = jnp.sum(kn * kn, axis=-1, keepdims=True) + kr_ss
        r = lax.rsqrt(ss * (1.0 / MLA_QK_DIM) + RMS_EPS)
        k_ref[h, :, :MLA_NOPE] = (kn * r * kg[:, :MLA_NOPE]).astype(BF16)
        k_ref[h, :, MLA_NOPE:] = rope(kr * r * kg[:, MLA_NOPE:]).astype(BF16)
        v_ref[h, :, :MLA_V] = kvall[:, base + MLA_NOPE:base + QK_PAD].astype(BF16)
        v_ref[h, :, MLA_V:] = ones_col


def _mla_prep(mla_in, pos3, freq, qn, wuq, kvn, wukv, qhn, khn):
    b, s, _ = mla_in.shape
    tm = min(TM_PREP, s)
    hd = MLA_HEADS
    return pl.pallas_call(
        _mla_prep_body,
        grid=(b, s // tm),
        in_specs=[
            pl.BlockSpec((None, tm, W_MLA), lambda i, j: (i, j, 0)),
            pl.BlockSpec((None, tm, 1), lambda i, j: (i, j, 0)),
            _const_spec(freq.shape), _const_spec(qn.shape), _const_spec(wuq.shape),
            _const_spec(kvn.shape), _const_spec(wukv.shape), _const_spec(qhn.shape), _const_spec(khn.shape),
        ],
        out_specs=[
            pl.BlockSpec((None, hd, tm, QK_PAD), lambda i, j: (i, 0, j, 0)),
            pl.BlockSpec((None, hd, tm, QK_PAD), lambda i, j: (i, 0, j, 0)),
            pl.BlockSpec((None, hd, tm, V_PAD), lambda i, j: (i, 0, j, 0)),
        ],
        out_shape=[
            jax.ShapeDtypeStruct((b, hd, s, QK_PAD), BF16),
            jax.ShapeDtypeStruct((b, hd, s, QK_PAD), BF16),
            jax.ShapeDtypeStruct((b, hd, s, V_PAD), BF16),
        ],
        compiler_params=_cparams(("parallel", "parallel")),
        name="mla_prep",
    )(mla_in, pos3, freq, qn, wuq, kvn, wukv, qhn, khn)


NEG_BIG = -1e30


def _attn_body(q_ref, k_ref, v_ref, o_ref, *, tq, tc, tkm):
    i = pl.program_id(2)
    n_chain = tq // tc
    qs = [q_ref[c * tc:(c + 1) * tc, :] for c in range(n_chain)]

    def steps(chains, starts, sizes, carries, mask):
        scores = [_dot_nt(qs[c], k_ref[pl.ds(st, sz), :]) for c, st, sz in zip(chains, starts, sizes)]
        probs, m_news = [], []
        for c, s in zip(chains, scores):
            if mask is not None:
                s = jnp.where(mask, s, NEG_BIG)
            m_new = jnp.maximum(carries[c][0], jnp.max(s, axis=-1, keepdims=True))
            probs.append(jnp.exp2(s - m_new).astype(BF16))
            m_news.append(m_new)
        out = list(carries)
        for c, st, sz, p, m_new in zip(chains, starts, sizes, probs, m_news):
            m, acc = carries[c]
            out[c] = (m_new, jnp.exp2(m - m_new) * acc + _dot(p, v_ref[pl.ds(st, sz), :]))
        return out

    all_chains = tuple(range(n_chain))

    def full_blocks(j, carries):
        start = pl.multiple_of(j * tkm, tkm)
        return tuple(steps(all_chains, (start,) * n_chain, (tkm,) * n_chain, carries, None))

    init = tuple((jnp.full((tc, 1), NEG_BIG, F32), jnp.zeros((tc, V_PAD), F32)) for _ in range(n_chain))
    carries = list(lax.fori_loop(0, i * (tq // tkm), full_blocks, init))
    tile_start = pl.multiple_of(i * tq, tq)
    if n_chain > 1:
        later = all_chains[1:]
        carries = steps(later, (tile_start,) * len(later), tuple(c * tc for c in later), carries, None)
    diag_mask = (lax.broadcasted_iota(jnp.int32, (tc, tc), 0) // CHUNK
                 >= lax.broadcasted_iota(jnp.int32, (tc, tc), 1) // CHUNK)
    carries = steps(all_chains, tuple(pl.multiple_of(tile_start + c * tc, tc) for c in all_chains),
                    (tc,) * n_chain, carries, diag_mask)
    for c in range(n_chain):
        acc = carries[c][1]
        o_ref[c * tc:(c + 1) * tc, :] = (acc[:, :MLA_V] / acc[:, MLA_V:MLA_V + 1]).astype(o_ref.dtype)


def _attention(q, k, v):
    b, hd, s, _ = q.shape
    tq = min(TQ, s)
    tc = min(TC_ATTN, tq)
    tkm = min(TK_MAIN, tq)
    return pl.pallas_call(
        functools.partial(_attn_body, tq=tq, tc=tc, tkm=tkm),
        grid=(b, hd, s // tq),
        in_specs=[
            pl.BlockSpec((None, None, tq, QK_PAD), lambda bi, h, i: (bi, h, i, 0)),
            pl.BlockSpec((None, None, s, QK_PAD), lambda bi, h, i: (bi, h, 0, 0)),
            pl.BlockSpec((None, None, s, V_PAD), lambda bi, h, i: (bi, h, 0, 0)),
        ],
        out_specs=pl.BlockSpec((None, tq, MLA_V), lambda bi, h, i: (bi, i, h)),
        out_shape=jax.ShapeDtypeStruct((b, s, MLA_WIDTH), BF16),
        compiler_params=_cparams(("parallel", "parallel", "arbitrary")),
        name="mla_attention",
    )(q, k, v)


def _causal_conv(x, halo_ref, w):
    t = x.shape[0]
    xe = jnp.concatenate([halo_ref[...], x], axis=0)
    halo_ref[...] = x[t - SUBLANES:, :]
    out = xe[SUBLANES:, :] * w[CONV_WIDTH - 1:CONV_WIDTH, :]
    for j in range(CONV_WIDTH - 1):
        shift = CONV_WIDTH - 1 - j
        out = out + xe[SUBLANES - shift:SUBLANES - shift + t, :] * w[j:j + 1, :]
    return out


def _lru_body(lru_ref, cw_ref, cb_ref, wa_ref, ba_ref, wx_ref, bx_ref, lam_ref, on_ref, y_ref, halo_ref, h_ref):
    @pl.when(pl.program_id(1) == 0)
    def _():
        halo_ref[...] = jnp.zeros_like(halo_ref)
        h_ref[...] = jnp.zeros_like(h_ref)

    blk = lru_ref[...].astype(F32)
    ts = blk.shape[0]
    x = blk[:, :LRU_WIDTH]
    gate = blk[:, LRU_WIDTH:]
    xc = _causal_conv(x, halo_ref, cw_ref[...]) + cb_ref[...]
    xb = xc.astype(BF16)
    r = _sigmoid(_dot(xb, wa_ref[...]) + ba_ref[...])
    ig = _sigmoid(_dot(xb, wx_ref[...]) + bx_ref[...])
    log_a = (-LRU_C) * r * _softplus(-lam_ref[...])
    a = jnp.exp(log_a)
    u = jnp.sqrt(-jnp.tanh(log_a) * (a * a + 1.0)) * (ig * xc)

    row = lax.broadcasted_iota(jnp.int32, (ts, LRU_WIDTH), 0)
    d = 1
    while d < ts:
        if d < SUBLANES:
            keep = row >= d
            a_prev = jnp.where(keep, pltpu.roll(a, d, 0), 1.0)
            u_prev = jnp.where(keep, pltpu.roll(u, d, 0), 0.0)
            u = a * u_prev + u
            a = a * a_prev
        else:
            u = jnp.concatenate([u[:d], a[d:] * u[:ts - d] + u[d:]], axis=0)
            a = jnp.concatenate([a[:d], a[d:] * a[:ts - d]], axis=0)
        d *= 2
    h = a * h_ref[...] + u
    h_ref[...] = h[ts - 1:ts, :]
    y = h * jax.nn.gelu(gate, approximate=True)
    y_ref[...] = _rms(y, on_ref[...]).astype(y_ref.dtype)


def _rglru(lru_in, cw, cb, wa, ba, wx, bx, lam, on):
    b, s, _ = lru_in.shape
    ts = min(TS_LRU, s)
    params = (cw, cb, wa, ba, wx, bx, lam, on)
    return pl.pallas_call(
        _lru_body,
        grid=(b, s // ts),
        in_specs=[pl.BlockSpec((None, ts, W_LRU), lambda i, j: (i, j, 0))] + [_const_spec(p.shape) for p in params],
        out_specs=pl.BlockSpec((None, ts, LRU_WIDTH), lambda i, j: (i, j, 0)),
        out_shape=jax.ShapeDtypeStruct((b, s, LRU_WIDTH), BF16),
        scratch_shapes=[pltpu.VMEM((SUBLANES, LRU_WIDTH), F32), pltpu.VMEM((1, LRU_WIDTH), F32)],
        compiler_params=_cparams(("parallel", "arbitrary")),
        name="rglru",
    )(lru_in, *params)


def _gdn_body(qkv_ref, z_ref, ab_ref, cw_ref, alog_ref, dtb_ref, on_ref, y_ref, halo_ref, state_ref):
    @pl.when(pl.program_id(1) == 0)
    def _():
        halo_ref[...] = jnp.zeros_like(halo_ref)
        state_ref[...] = jnp.zeros_like(state_ref)

    x = qkv_ref[...].astype(F32)
    t = x.shape[0]
    c = _causal_conv(x, halo_ref, cw_ref[...])
    c = c * _sigmoid(c)

    ab = ab_ref[...]
    lane = lax.broadcasted_iota(jnp.int32, (t, LANES), 1)
    g = jnp.where(lane < GDN_HEADS, -jnp.exp(alog_ref[...]) * _softplus(ab + dtb_ref[...]), 0.0)
    beta_all = _sigmoid(ab)

    pos = lax.broadcasted_iota(jnp.int32, (t, LANES), 0) % CHUNK
    gc = g
    rc = g
    d = 1
    while d < CHUNK:
        gc = gc + jnp.where(pos >= d, pltpu.roll(gc, d, 0), 0.0)
        rc = rc + jnp.where(pos + d < CHUNK, pltpu.roll(rc, t - d, 0), 0.0)
        d *= 2
    tail = rc - g
    gct = gc.T

    blk = min(GDN_BLK, t)
    n_chunks = t // CHUNK
    per_blk = blk // CHUNK
    ri = lax.broadcasted_iota(jnp.int32, (blk, blk), 0)
    ci = lax.broadcasted_iota(jnp.int32, (blk, blk), 1)
    same = (ri // CHUNK) == (ci // CHUNK)
    causal = jnp.logical_and(same, ri >= ci)
    strict = jnp.logical_and(same, ri > ci)
    eye = jnp.where(ri == ci, 1.0, 0.0)
    col_chunk = lax.broadcasted_iota(jnp.int32, (GDN_DK, blk), 1) // CHUNK
    n_levels = int(math.log2(CHUNK))

    units = [(sb, h) for sb in range(t // blk) for h in range(GDN_HEADS)]
    qs, ks, kbs, decays, gccs, rhss = [], [], [], [], [], []
    for sb, h in units:
        rows = slice(sb * blk, (sb + 1) * blk)
        q = c[rows, h * GDN_DK:(h + 1) * GDN_DK]
        k = c[rows, GDN_WIDTH + h * GDN_DK:GDN_WIDTH + (h + 1) * GDN_DK]
        v = c[rows, 2 * GDN_WIDTH + h * GDN_DV:2 * GDN_WIDTH + (h + 1) * GDN_DV]
        q = q * (lax.rsqrt(jnp.sum(q * q, axis=-1, keepdims=True) + RMS_EPS) * (GDN_DK ** -0.5))
        k = k * lax.rsqrt(jnp.sum(k * k, axis=-1, keepdims=True) + RMS_EPS)
        beta = beta_all[rows, GDN_HEADS + h:GDN_HEADS + h + 1]
        gcc = gc[rows, h:h + 1]
        kb = k * beta
        qs.append(q)
        ks.append(k)
        kbs.append(kb)
        gccs.append(gcc)
        decays.append(jnp.exp(jnp.where(causal, gcc - gct[h:h + 1, rows], NEG_BIG)))
        rhss.append(jnp.concatenate([v * beta, kb * jnp.exp(gcc)], axis=1).astype(BF16))

    ids = range(len(units))
    kqs = [_dot_nt(jnp.concatenate([kbs[u], qs[u]], axis=0).astype(BF16), ks[u].astype(BF16)) for u in ids]
    a_lows = [jnp.where(strict, kqs[u][:blk] * decays[u], 0.0) for u in ids]
    attns = [kqs[u][blk:] * decays[u] for u in ids]

    invs = [eye - a_lows[u] for u in ids]
    pows = []
    for u in ids:
        a16 = a_lows[u].astype(BF16)
        pows.append(_dot(a16, a16))
    for level in range(1, n_levels):
        for u in ids:
            p16 = pows[u].astype(BF16)
            if level < n_levels - 1:
                both = _dot(jnp.concatenate([invs[u].astype(BF16), p16], axis=0), p16)
                invs[u] = invs[u] + both[:blk]
                pows[u] = both[blk:]
            else:
                invs[u] = invs[u] + _dot(invs[u].astype(BF16), p16)
    uws = [_dot(invs[u].astype(BF16), rhss[u]).astype(BF16) for u in ids]

    prods = []
    for u, (sb, h) in enumerate(units):
        ke_t = (ks[u] * jnp.exp(tail[sb * blk:(sb + 1) * blk, h:h + 1])).T
        parts = []
        for n in range(per_blk):
            parts.append(jnp.where(col_chunk == n, ke_t, 0.0))
            parts.append(attns[u][n * CHUNK:(n + 1) * CHUNK, :])
        prods.append(_dot(jnp.concatenate(parts, axis=0).astype(BF16), uws[u]))
    terms = [[None] * n_chunks for _ in range(GDN_HEADS)]
    for u, (sb, h) in enumerate(units):
        prod = prods[u]
        qd = qs[u] * jnp.exp(gccs[u])
        for n in range(per_blk):
            r0 = n * (GDN_DK + CHUNK)
            rows = slice(n * CHUNK, (n + 1) * CHUNK)
            terms[h][sb * per_blk + n] = dict(
                lhs=jnp.concatenate([-prod[r0:r0 + GDN_DK, GDN_DV:],
                                     qd[rows] - prod[r0 + GDN_DK:r0 + GDN_DK + CHUNK, GDN_DV:]], axis=0).astype(BF16),
                add_s=prod[r0:r0 + GDN_DK, :GDN_DV],
                add_o=prod[r0 + GDN_DK:r0 + GDN_DK + CHUNK, :GDN_DV],
                dec=jnp.exp(gccs[u][(n + 1) * CHUNK - 1:(n + 1) * CHUNK, :]))

    states = [state_ref[h] for h in range(GDN_HEADS)]
    outs = [[] for _ in range(GDN_HEADS)]
    rows_h = GDN_DK + CHUNK
    for n in range(n_chunks):
        for h0 in range(0, GDN_HEADS, 2):
            pair = (h0, h0 + 1)
            lhs = jnp.concatenate([terms[h][n]["lhs"] for h in pair], axis=0)
            rhs = jnp.concatenate([states[h] for h in pair], axis=1).astype(BF16)
            both = _dot(lhs, rhs)
            for idx, h in enumerate(pair):
                tm = terms[h][n]
                blk = both[idx * rows_h:(idx + 1) * rows_h, idx * GDN_DV:(idx + 1) * GDN_DV]
                outs[h].append(blk[GDN_DK:] + tm["add_o"])
                states[h] = states[h] * tm["dec"] + blk[:GDN_DK] + tm["add_s"]
    on = on_ref[...]
    for h in range(GDN_HEADS):
        state_ref[h] = states[h]
        o = jnp.concatenate(outs[h], axis=0)
        zz = z_ref[:, h * GDN_DV:(h + 1) * GDN_DV].astype(F32)
        y_ref[:, h * GDN_DV:(h + 1) * GDN_DV] = (_rms(o, on) * (zz * _sigmoid(zz))).astype(y_ref.dtype)


def _gdn(qkv, z, ab, cw, alog, dtb, on):
    b, s, _ = qkv.shape
    t = min(T_GDN, s)
    params = (cw, alog, dtb, on)
    return pl.pallas_call(
        _gdn_body,
        grid=(b, s // t),
        in_specs=[
            pl.BlockSpec((None, t, W_QKV), lambda i, j: (i, j, 0)),
            pl.BlockSpec((None, t, W_Z), lambda i, j: (i, j, 0)),
            pl.BlockSpec((None, t, W_AB), lambda i, j: (i, j, 0)),
        ] + [_const_spec(p.shape) for p in params],
        out_specs=pl.BlockSpec((None, t, GDN_WIDTH), lambda i, j: (i, j, 0)),
        out_shape=jax.ShapeDtypeStruct((b, s, GDN_WIDTH), BF16),
        scratch_shapes=[pltpu.VMEM((SUBLANES, W_QKV), F32), pltpu.VMEM((GDN_HEADS, GDN_DK, GDN_DV), F32)],
        compiler_params=_cparams(("parallel", "arbitrary")),
        name="gdn",
    )(qkv, z, ab, *params)


def _outproj_body(*refs, moe):
    if moe:
        x_ref, om_ref, yl_ref, yg_ref, mon_ref, wo_ref, fn_ref, rw_ref, xo_ref, h_ref, route_ref = refs
    else:
        x_ref, om_ref, yl_ref, yg_ref, mon_ref, wo_ref, fn_ref, xo_ref, h_ref = refs
    ym = _rms(om_ref[...].astype(F32), mon_ref[...]).astype(BF16)
    mixed = jnp.concatenate([ym, yl_ref[...], yg_ref[...]], axis=1)
    xn = x_ref[...] + _dot(mixed, wo_ref[...])
    xo_ref[...] = xn
    h = _rms(xn, fn_ref[...])
    h_ref[...] = h.astype(h_ref.dtype)
    if moe:
        tm = h.shape[0]
        h_hi = h.astype(BF16)
        h_lo = (h - h_hi.astype(F32)).astype(BF16)
        rw = rw_ref[...]
        rw_hi = rw.astype(BF16)
        rw_lo = (rw - rw_hi.astype(F32)).astype(BF16)
        hi_part = _dot(h_hi, jnp.concatenate([rw_hi, rw_lo], axis=1))
        logits = hi_part[:, :LANES] + hi_part[:, LANES:] + _dot(h_lo, rw_hi)
        lane = lax.broadcasted_iota(jnp.int32, (tm, LANES), 1)
        logits = jnp.where(lane < N_EXPERTS, logits, -jnp.inf)
        m1 = jnp.max(logits, axis=-1, keepdims=True)
        i1 = jnp.min(jnp.where(logits == m1, lane, LANES), axis=-1, keepdims=True)
        rest = jnp.where(lane == i1, -jnp.inf, logits)
        m2 = jnp.max(rest, axis=-1, keepdims=True)
        i2 = jnp.min(jnp.where(rest == m2, lane, LANES), axis=-1, keepdims=True)
        e2 = jnp.exp(m2 - m1)
        g1 = 1.0 / (1.0 + e2)
        g2 = e2 * g1
        route = jnp.where(lane == 0, i1.astype(F32),
                          jnp.where(lane == 1, i2.astype(F32),
                                    jnp.where(lane == 2, g1, jnp.where(lane == 3, g2, 0.0))))
        route_ref[...] = route


def _outproj(x, om, yl, yg, mon, wo, fn, rw=None):
    b, s, d = x.shape
    tm = min(TM_OUT, s)
    moe = rw is not None
    row = lambda w: pl.BlockSpec((None, tm, w), lambda i, j: (i, j, 0))
    in_specs = [row(d), row(MLA_WIDTH), row(LRU_WIDTH), row(GDN_WIDTH),
                _const_spec(mon.shape), _const_spec(wo.shape), _const_spec(fn.shape)]
    args = [x, om, yl, yg, mon, wo, fn]
    out_specs = [row(d), row(d)]
    out_shape = [jax.ShapeDtypeStruct((b, s, d), F32), jax.ShapeDtypeStruct((b, s, d), F32 if moe else BF16)]
    if moe:
        in_specs.append(_const_spec(rw.shape))
        args.append(rw)
        out_specs.append(row(LANES))
        out_shape.append(jax.ShapeDtypeStruct((b, s, LANES), F32))
    return pl.pallas_call(
        functools.partial(_outproj_body, moe=moe),
        grid=(b, s // tm),
        in_specs=in_specs,
        out_specs=out_specs,
        out_shape=out_shape,
        compiler_params=_cparams(("parallel", "parallel")),
        name="outproj_moe" if moe else "outproj",
    )(*args)


def _dense_ffn_body(x_ref, h_ref, wg_ref, wu_ref, wd_ref, o_ref, a_ref):
    h = h_ref[...]
    ff = wg_ref.shape[1]
    for lo in range(0, ff, FC_DENSE):
        hi = min(lo + FC_DENSE, ff)
        g = _dot(h, wg_ref[:, lo:hi])
        u = _dot(h, wu_ref[:, lo:hi])
        a_ref[:, lo:hi] = (g * _sigmoid(g) * u).astype(BF16)
    o_ref[...] = x_ref[...] + _dot(a_ref[...], wd_ref[...])


def _dense_ffn(x, h, wg, wu, wd):
    n, d = x.shape
    tm = min(TM_FFN, n)
    ff = wg.shape[1]
    resident = lambda shape: pl.BlockSpec(shape, lambda i: (0, 0), pipeline_mode=pl.Buffered(1))
    return pl.pallas_call(
        _dense_ffn_body,
        grid=(n // tm,),
        in_specs=[
            pl.BlockSpec((tm, d), lambda i: (i, 0)),
            pl.BlockSpec((tm, d), lambda i: (i, 0)),
            resident(wg.shape), resident(wu.shape), resident(wd.shape),
        ],
        out_specs=pl.BlockSpec((tm, d), lambda i: (i, 0)),
        out_shape=jax.ShapeDtypeStruct((n, d), F32),
        scratch_shapes=[pltpu.VMEM((tm, ff), BF16)],
        compiler_params=_cparams(("parallel",)),
        name="dense_ffn",
    )(x, h, wg, wu, wd)


def _row_copy(src_hbm, dst_ref, src_row, dst_row, sem):
    return pltpu.make_async_copy(src_hbm.at[pl.ds(src_row, 1), :], dst_ref.at[pl.ds(dst_row, 1), :], sem)


def _wait_rows(src_hbm, dst_ref, sem):
    pltpu.make_async_copy(src_hbm.at[pl.ds(0, dst_ref.shape[0]), :], dst_ref, sem).wait()


def _dispatch_body(pos_ref, zt_ref, h_hbm, xs_hbm, buf_ref, zero_ref, load_sem, row_sem, zero_sem, *, tc, tm):
    i = pl.program_id(0)
    n_steps = pl.num_programs(0)
    slot = i % 2

    @pl.when(i == 0)
    def _zero_padding_tiles():
        zero_ref[...] = jnp.zeros_like(zero_ref)
        for phase in ("start", "wait"):
            for j in range(zt_ref.shape[0]):
                tile = zt_ref[j]

                @pl.when(tile >= 0)
                def _():
                    dst = xs_hbm.at[pl.ds(pl.multiple_of(jnp.maximum(tile, 0) * tm, tm), tm), :]
                    copy = pltpu.make_async_copy(zero_ref, dst, zero_sem)
                    copy.start() if phase == "start" else copy.wait()

    def block_load(step, dst_slot):
        return pltpu.make_async_copy(h_hbm.at[pl.ds(pl.multiple_of(step * tc, tc), tc), :], buf_ref.at[dst_slot],
                                     load_sem.at[dst_slot])

    def wait_rows_out(src_slot):
        for _ in range(TOP_K):
            pltpu.make_async_copy(buf_ref.at[src_slot], xs_hbm.at[pl.ds(0, tc), :], row_sem.at[src_slot]).wait()

    @pl.when(i == 0)
    def _first_load():
        block_load(0, 0).start()

    block_load(i, slot).wait()
    base = i * tc
    for r in range(tc):
        for k in range(TOP_K):
            pltpu.make_async_copy(buf_ref.at[slot, pl.ds(r, 1), :],
                                  xs_hbm.at[pl.ds(pos_ref[(base + r) * TOP_K + k], 1), :],
                                  row_sem.at[slot]).start(priority=k)

    @pl.when(i > 0)
    def _other_slot_free():
        wait_rows_out(1 - slot)

    @pl.when(i + 1 < n_steps)
    def _next_load():
        block_load(i + 1, 1 - slot).start()

    @pl.when(i == n_steps - 1)
    def _drain():
        wait_rows_out(slot)


def _moe_dispatch(pos_flat, zero_tiles, h, n_rows, tm):
    n, d = h.shape
    tc = min(TC_DISPATCH, n)
    grid_spec = pltpu.PrefetchScalarGridSpec(
        num_scalar_prefetch=2,
        grid=(n // tc,),
        in_specs=[pl.BlockSpec(memory_space=pl.ANY)],
        out_specs=pl.BlockSpec(memory_space=pl.ANY),
        scratch_shapes=[pltpu.VMEM((2, tc, d), F32), pltpu.VMEM((tm, d), F32), pltpu.SemaphoreType.DMA((2,)),
                        pltpu.SemaphoreType.DMA((2,)), pltpu.SemaphoreType.DMA(())],
    )
    return pl.pallas_call(
        functools.partial(_dispatch_body, tc=tc, tm=tm),
        grid_spec=grid_spec,
        out_shape=jax.ShapeDtypeStruct((n_rows, d), F32),
        compiler_params=_cparams(("arbitrary",)),
        name="moe_dispatch",
    )(pos_flat, zero_tiles, h)


def _moe_body(te_ref, tv_ref, x_ref, wg_ref, wu_ref, wd_ref, y_ref, a_ref):
    del te_ref
    i = pl.program_id(0)
    f = pl.program_id(1)

    @pl.when(f == 0)
    def _zero():
        y_ref[...] = jnp.zeros_like(y_ref)

    @pl.when(tv_ref[i] > 0)
    def _compute():
        x = x_ref[...].astype(BF16)
        tf = wg_ref.shape[1]
        for lo in range(0, tf, FC_MOE):
            hi = min(lo + FC_MOE, tf)
            g = _dot(x, wg_ref[:, lo:hi])
            u = _dot(x, wu_ref[:, lo:hi])
            a_ref[:, lo:hi] = (g * _sigmoid(g) * u).astype(BF16)
        y_ref[...] += _dot(a_ref[...], wd_ref[...])


def _moe_experts(tile_expert, tile_valid, xs, wg, wu, wd, tm):
    n_rows, d = xs.shape
    n_tiles = tile_expert.shape[0]
    ff = wg.shape[2]
    tf = TF_MOE if ff % TF_MOE == 0 else ff
    nf = ff // tf

    def f_idx(i, f, tv):
        return jnp.where(tv[i] > 0, f, nf - 1)

    grid_spec = pltpu.PrefetchScalarGridSpec(
        num_scalar_prefetch=2,
        grid=(n_tiles, nf),
        in_specs=[
            pl.BlockSpec((tm, d), lambda i, f, te, tv: (i, 0)),
            pl.BlockSpec((None, d, tf), lambda i, f, te, tv: (te[i], 0, f_idx(i, f, tv))),
            pl.BlockSpec((None, d, tf), lambda i, f, te, tv: (te[i], 0, f_idx(i, f, tv))),
            pl.BlockSpec((None, tf, d), lambda i, f, te, tv: (te[i], f_idx(i, f, tv), 0)),
        ],
        out_specs=pl.BlockSpec((tm, d), lambda i, f, te, tv: (i, 0)),
        scratch_shapes=[pltpu.VMEM((tm, tf), BF16)],
    )
    return pl.pallas_call(
        _moe_body,
        grid_spec=grid_spec,
        out_shape=jax.ShapeDtypeStruct((n_rows, d), F32),
        compiler_params=_cparams(("arbitrary", "arbitrary")),
        name="moe_experts",
    )(tile_expert, tile_valid, xs, wg, wu, wd)


def _combine_body(pos_ref, x_ref, route_ref, y_hbm, o_ref, buf_ref, sem, *, tc):
    i = pl.program_id(0)
    slot = i % 2

    def gather_row(step, dst_slot, r):
        for k in range(TOP_K):
            _row_copy(y_hbm, buf_ref.at[dst_slot, k], pos_ref[(step * tc + r) * TOP_K + k], r,
                      sem.at[dst_slot]).start(priority=k)

    @pl.when(i == 0)
    def _first_gather():
        def issue(r, carry):
            gather_row(0, 0, r)
            return carry
        lax.fori_loop(0, tc, issue, 0, unroll=GATHER_UNROLL)

    @pl.when(i + 1 < pl.num_programs(0))
    def _prefetch():
        for r in range(tc):
            gather_row(i + 1, 1 - slot, r)

    for k in range(TOP_K):
        _wait_rows(y_hbm, buf_ref.at[slot, k], sem.at[slot])
    acc = x_ref[...]
    route = route_ref[...]
    for k in range(TOP_K):
        acc = acc + route[:, TOP_K + k:TOP_K + k + 1] * buf_ref[slot, k]
    o_ref[...] = acc


def _moe_combine(pos_flat, x, route, y_sorted):
    n, d = x.shape
    tc = min(TC_COMB, n)
    grid_spec = pltpu.PrefetchScalarGridSpec(
        num_scalar_prefetch=1,
        grid=(n // tc,),
        in_specs=[pl.BlockSpec((tc, d), lambda i, pos: (i, 0)), pl.BlockSpec((tc, LANES), lambda i, pos: (i, 0)),
                  pl.BlockSpec(memory_space=pl.ANY)],
        out_specs=pl.BlockSpec((tc, d), lambda i, pos: (i, 0)),
        scratch_shapes=[pltpu.VMEM((2, TOP_K, tc, d), F32), pltpu.SemaphoreType.DMA((2,))],
    )
    return pl.pallas_call(
        functools.partial(_combine_body, tc=tc),
        grid_spec=grid_spec,
        out_shape=jax.ShapeDtypeStruct((n, d), F32),
        compiler_params=_cparams(("arbitrary",)),
        name="moe_combine",
    )(pos_flat, x, route, y_sorted)


def _moe_ffn(x, h, route, wg, wu, wd):
    n, d = x.shape
    tm = min(TM_MOE, n)
    experts = route[:, :TOP_K].astype(jnp.int32).reshape(-1)
    onehot = (experts[:, None] == jnp.arange(N_EXPERTS, dtype=jnp.int32)[None, :]).astype(jnp.int32)
    rank = jnp.sum((jnp.cumsum(onehot, axis=0) - onehot) * onehot, axis=1)
    counts = jnp.sum(onehot, axis=0)
    padded = ((counts + tm - 1) // tm) * tm
    ends = jnp.cumsum(padded)
    starts = ends - padded
    pos = starts[experts] + rank
    n_tiles = (n * TOP_K) // tm + N_EXPERTS
    tile_start = jnp.arange(n_tiles, dtype=jnp.int32) * tm
    tile_valid = (tile_start < ends[-1]).astype(jnp.int32)
    tile_expert = jnp.minimum(jnp.sum((tile_start[:, None] >= ends[None, :]).astype(jnp.int32), axis=1), N_EXPERTS - 1)
    last_expert = jnp.max(jnp.where(counts > 0, jnp.arange(N_EXPERTS, dtype=jnp.int32), 0))
    tile_expert = jnp.where(tile_valid > 0, tile_expert, last_expert).astype(jnp.int32)
    pos = pos.astype(jnp.int32)
    group_last = jnp.where(counts > 0, ends // tm - 1, -1)
    tail = ends[-1] // tm + jnp.arange(N_EXPERTS, dtype=jnp.int32)
    tail = jnp.where(tail < n_tiles, tail, -1)
    zero_tiles = jnp.concatenate([group_last, tail]).astype(jnp.int32)
    xs = _moe_dispatch(pos, zero_tiles, h, n_tiles * tm, tm)
    y_sorted = _moe_experts(tile_expert, tile_valid, xs, wg, wu, wd, tm)
    return _moe_combine(pos, x, route, y_sorted)


def _pad_cols(w, width):
    return jnp.pad(w, ((0, 0), (0, width - w.shape[1])))


def _prep_w_in(w):
    n_mla = MLA_Q_RANK + MLA_KV_RANK + MLA_ROPE
    n_main = n_mla + W_LRU + W_QKV + W_Z
    return jnp.concatenate([_pad_cols(w[:, :n_mla], W_MLA), w[:, n_mla:n_main], _pad_cols(w[:, n_main:], W_AB)],
                           axis=1).astype(BF16)


def _prep_w_uq(w):
    w = w.reshape(MLA_Q_RANK, MLA_HEADS, MLA_QK_DIM)
    w = jnp.pad(w, ((0, 0), (0, 0), (0, QK_PAD - MLA_QK_DIM)))
    return w.reshape(MLA_Q_RANK, MLA_HEADS * QK_PAD).astype(BF16)


def _block_diag(w):
    g, bi, bo = w.shape
    eye = jnp.eye(g, dtype=w.dtype)
    return (eye[:, None, :, None] * w[:, :, None, :]).reshape(g * bi, g * bo).astype(BF16)


def _row(v, width=None):
    v = v.reshape(1, -1).astype(F32)
    return v if width is None else _pad_cols(v, width)


def kernel(x, positions, mix_norm, w_in, mla_q_norm, mla_w_uq, mla_kv_norm, mla_w_ukv, mla_q_head_norm, mla_k_head_norm, mla_out_norm, lru_conv_w, lru_conv_b, lru_w_a, lru_b_a, lru_w_x, lru_b_x, lru_lambda, lru_out_norm, gdn_conv_w, gdn_a_log, gdn_dt_bias, gdn_out_norm, w_out, ffn_norm, dense_w_gate, dense_w_up, dense_w_down, router_w, moe_w_gate, moe_w_up, moe_w_down):
    b, s, d = x.shape
    depth = w_in.shape[0]
    pos3 = positions.reshape(b, s, 1)
    inv_freq = ROPE_THETA ** (-jnp.arange(0, MLA_ROPE, 2, dtype=F32) / MLA_ROPE)
    freq = _row(jnp.concatenate([inv_freq, inv_freq]), LANES)

    for layer in range(depth):
        mla_in, lru_in, qkv_in, z_in, ab_in = _inproj(x, _row(mix_norm[layer]), _prep_w_in(w_in[layer]))

        q, k, v = _mla_prep(
            mla_in, pos3, freq, _row(mla_q_norm[layer]), _prep_w_uq(mla_w_uq[layer]), _row(mla_kv_norm[layer]),
            mla_w_ukv[layer].astype(BF16), _row(mla_q_head_norm[layer], QK_PAD), _row(mla_k_head_norm[layer], QK_PAD))
        o_mla = _attention(q, k, v)

        y_lru = _rglru(
            lru_in, lru_conv_w[layer], _row(lru_conv_b[layer]), _block_diag(lru_w_a[layer]), _row(lru_b_a[layer]),
            _block_diag(lru_w_x[layer]), _row(lru_b_x[layer]), _row(lru_lambda[layer]), _row(lru_out_norm[layer]))

        y_gdn = _gdn(qkv_in, z_in, ab_in, gdn_conv_w[layer], _row(gdn_a_log[layer], LANES),
                     _row(gdn_dt_bias[layer], LANES), _row(gdn_out_norm[layer]))

        wo = w_out[layer].astype(BF16)
        if layer % 2 == 0:
            x, h = _outproj(x, o_mla, y_lru, y_gdn, _row(mla_out_norm[layer]), wo, _row(ffn_norm[layer]))
            e = layer // 2
            x = _dense_ffn(x.reshape(b * s, d), h.reshape(b * s, d), dense_w_gate[e].astype(BF16),
                           dense_w_up[e].astype(BF16), dense_w_down[e].astype(BF16)).reshape(b, s, d)
        else:
            e = layer // 2
            x, h, route = _outproj(x, o_mla, y_lru, y_gdn, _row(mla_out_norm[layer]), wo, _row(ffn_norm[layer]),
                                   _pad_cols(router_w[e].astype(F32), LANES))
            x = _moe_ffn(x.reshape(b * s, d), h.reshape(b * s, d), route.reshape(b * s, LANES),
                         moe_w_gate[e].astype(BF16), moe_w_up[e].astype(BF16),
                         moe_w_down[e].astype(BF16)).reshape(b, s, d)
    return x
```

```python
import functools
import math

import jax
import jax.numpy as jnp
from jax import lax
from jax.experimental import pallas as pl
from jax.experimental.pallas import tpu as pltpu

F32 = jnp.float32
BF16 = jnp.bfloat16

D_MODEL = 1024
CHUNK = 64
RMS_EPS = 1e-6
CONV_WIDTH = 4
MLA_HEADS = 4
MLA_NOPE = 128
MLA_ROPE = 64
MLA_V = 128
MLA_QK_DIM = MLA_NOPE + MLA_ROPE
MLA_Q_RANK = 512
MLA_KV_RANK = 256
MLA_WIDTH = MLA_HEADS * MLA_V
MLA_SCALE = MLA_QK_DIM ** -0.5
ROPE_THETA = 10000.0
LRU_WIDTH = 512
LRU_BLOCKS = 8
LRU_C = 8.0
GDN_HEADS = 4
GDN_DK = 128
GDN_DV = 128
GDN_WIDTH = GDN_HEADS * GDN_DV
N_EXPERTS = 8
TOP_K = 2

LANES = 128
SUBLANES = 8
QK_PAD = 256
V_PAD = 256
LOG2_E = math.log2(math.e)
VMEM_LIMIT = 56 * 1024 * 1024

W_MLA = MLA_Q_RANK + MLA_KV_RANK + LANES
W_LRU = 2 * LRU_WIDTH
W_QKV = 3 * GDN_WIDTH
W_Z = GDN_WIDTH
W_AB = LANES
IN_GROUPS = (W_MLA, W_LRU, W_QKV, W_Z, W_AB)

TM_PROJ = 512
TM_PREP = 1024
TQ = 2048
TC_ATTN = 512
TK_MAIN = 2048
TS_LRU = 256
T_GDN = 512
GDN_BLK = 256
TM_OUT = 1024
TM_FFN = 1024
FC_DENSE = 1024
TM_MOE = 512
TF_MOE = 1792
FC_MOE = 1024
TC_DISPATCH = 1024
TC_COMB = 512
GATHER_UNROLL = 16


def _cparams(sem):
    return pltpu.CompilerParams(dimension_semantics=sem, vmem_limit_bytes=VMEM_LIMIT)


def _rms(x, gain):
    return x * lax.rsqrt(jnp.mean(x * x, axis=-1, keepdims=True) + RMS_EPS) * gain


def _sigmoid(x):
    return 1.0 / (1.0 + jnp.exp(-x))


def _softplus(x):
    return jnp.maximum(x, 0.0) + jnp.log1p(jnp.exp(-jnp.abs(x)))


def _dot(a, b, **kw):
    return jnp.dot(a, b, preferred_element_type=F32, **kw)


def _dot_nt(a, b):
    return lax.dot_general(a, b, (((1,), (1,)), ((), ())), preferred_element_type=F32)


def _const_spec(shape):
    nd = len(shape)
    return pl.BlockSpec(shape, lambda *_: (0,) * nd)


def _inproj_body(x_ref, g_ref, w_ref, mla_ref, lru_ref, qkv_ref, z_ref, ab_ref):
    h = _rms(x_ref[...], g_ref[...]).astype(BF16)
    proj = _dot(h, w_ref[...])
    off = 0
    for ref in (mla_ref, lru_ref, qkv_ref, z_ref, ab_ref):
        width = ref.shape[-1]
        ref[...] = proj[:, off:off + width].astype(ref.dtype)
        off += width


def _inproj(x, gain, w_pad):
    b, s, d = x.shape
    tm = min(TM_PROJ, s)
    dtypes = (BF16, BF16, BF16, BF16, F32)
    return pl.pallas_call(
        _inproj_body,
        grid=(b, s // tm),
        in_specs=[
            pl.BlockSpec((None, tm, d), lambda i, j: (i, j, 0)),
            _const_spec((1, d)),
            _const_spec(w_pad.shape),
        ],
        out_specs=[pl.BlockSpec((None, tm, w), lambda i, j: (i, j, 0)) for w in IN_GROUPS],
        out_shape=[jax.ShapeDtypeStruct((b, s, w), dt) for w, dt in zip(IN_GROUPS, dtypes)],
        compiler_params=_cparams(("parallel", "parallel")),
        name="inproj",
    )(x, gain, w_pad)


def _mla_prep_body(mla_ref, pos_ref, freq_ref, qn_ref, wuq_ref, kvn_ref, wukv_ref, qhn_ref, khn_ref,
                   q_ref, k_ref, v_ref):
    m = mla_ref[...].astype(F32)
    tm = m.shape[0]
    cq = m[:, :MLA_Q_RANK]
    ckv = m[:, MLA_Q_RANK:MLA_Q_RANK + MLA_KV_RANK]
    kr = m[:, MLA_Q_RANK + MLA_KV_RANK:]
    qall = _dot(_rms(cq, qn_ref[...]).astype(BF16), wuq_ref[...])
    kvall = _dot(_rms(ckv, kvn_ref[...]).astype(BF16), wukv_ref[...])

    ang = pos_ref[...].astype(F32) * freq_ref[...]
    cos = jnp.cos(ang)
    sin = jnp.sin(ang)
    lane = lax.broadcasted_iota(jnp.int32, (tm, LANES), 1)
    half = MLA_ROPE // 2
    first = lane < half
    sin_signed = jnp.where(first, -sin, sin)

    def rope(x):
        swapped = jnp.where(first, pltpu.roll(x, LANES - half, 1), pltpu.roll(x, half, 1))
        return x * cos + swapped * sin_signed

    qg = qhn_ref[...]
    kg = khn_ref[...]
    kr_ss = jnp.sum(kr * kr, axis=-1, keepdims=True)
    ones_col = jnp.where(lane == 0, 1.0, 0.0).astype(BF16)
    for h in range(MLA_HEADS):
        base = h * QK_PAD
        qn = qall[:, base:base + MLA_NOPE]
        qr = qall[:, base + MLA_NOPE:base + QK_PAD]
        ss = jnp.sum(qn * qn, axis=-1, keepdims=True) + jnp.sum(qr * qr, axis=-1, keepdims=True)
        r = lax.rsqrt(ss * (1.0 / MLA_QK_DIM) + RMS_EPS) * (MLA_SCALE * LOG2_E)
        q_ref[h, :, :MLA_NOPE] = (qn * r * qg[:, :MLA_NOPE]).astype(BF16)
        q_ref[h, :, MLA_NOPE:] = rope(qr * r * qg[:, MLA_NOPE:]).astype(BF16)

        kn = kvall[:, base:base + MLA_NOPE]
        ss = jnp.sum(kn * kn, axis=-1, keepdims=True) + kr_ss
        r = lax.rsqrt(ss * (1.0 / MLA_QK_DIM) + RMS_EPS)
        k_ref[h, :, :MLA_NOPE] = (kn * r * kg[:, :MLA_NOPE]).astype(BF16)
        k_ref[h, :, MLA_NOPE:] = rope(kr * r * kg[:, MLA_NOPE:]).astype(BF16)
        v_ref[h, :, :MLA_V] = kvall[:, base + MLA_NOPE:base + QK_PAD].astype(BF16)
        v_ref[h, :, MLA_V:] = ones_col


def _mla_prep(mla_in, pos3, freq, qn, wuq, kvn, wukv, qhn, khn):
    b, s, _ = mla_in.shape
    tm = min(TM_PREP, s)
    hd = MLA_HEADS
    return pl.pallas_call(
        _mla_prep_body,
        grid=(b, s // tm),
        in_specs=[
            pl.BlockSpec((None, tm, W_MLA), lambda i, j: (i, j, 0)),
            pl.BlockSpec((None, tm, 1), lambda i, j: (i, j, 0)),
            _const_spec(freq.shape), _const_spec(qn.shape), _const_spec(wuq.shape),
            _const_spec(kvn.shape), _const_spec(wukv.shape), _const_spec(qhn.shape), _const_spec(khn.shape),
        ],
        out_specs=[
            pl.BlockSpec((None, hd, tm, QK_PAD), lambda i, j: (i, 0, j, 0)),
            pl.BlockSpec((None, hd, tm, QK_PAD), lambda i, j: (i, 0, j, 0)),
            pl.BlockSpec((None, hd, tm, V_PAD), lambda i, j: (i, 0, j, 0)),
        ],
        out_shape=[
            jax.ShapeDtypeStruct((b, hd, s, QK_PAD), BF16),
            jax.ShapeDtypeStruct((b, hd, s, QK_PAD), BF16),
            jax.ShapeDtypeStruct((b, hd, s, V_PAD), BF16),
        ],
        compiler_params=_cparams(("parallel", "parallel")),
        name="mla_prep",
    )(mla_in, pos3, freq, qn, wuq, kvn, wukv, qhn, khn)


NEG_BIG = -1e30


def _attn_body(q_ref, k_ref, v_ref, o_ref, *, tq, tc, tkm):
    i = pl.program_id(2)
    n_chain = tq // tc
    qs = [q_ref[c * tc:(c + 1) * tc, :] for c in range(n_chain)]

    def steps(chains, starts, sizes, carries, mask):
        scores = [_dot_nt(qs[c], k_ref[pl.ds(st, sz), :]) for c, st, sz in zip(chains, starts, sizes)]
        probs, m_news = [], []
        for c, s in zip(chains, scores):
            if mask is not None:
                s = jnp.where(mask, s, NEG_BIG)
            m_new = jnp.maximum(carries[c][0], jnp.max(s, axis=-1, keepdims=True))
            probs.append(jnp.exp2(s - m_new).astype(BF16))
            m_news.append(m_new)
        out = list(carries)
        for c, st, sz, p, m_new in zip(chains, starts, sizes, probs, m_news):
            m, acc = carries[c]
            out[c] = (m_new, jnp.exp2(m - m_new) * acc + _dot(p, v_ref[pl.ds(st, sz), :]))
        return out

    all_chains = tuple(range(n_chain))

    def full_blocks(j, carries):
        start = pl.multiple_of(j * tkm, tkm)
        return tuple(steps(all_chains, (start,) * n_chain, (tkm,) * n_chain, carries, None))

    init = tuple((jnp.full((tc, 1), NEG_BIG, F32), jnp.zeros((tc, V_PAD), F32)) for _ in range(n_chain))
    carries = list(lax.fori_loop(0, i * (tq // tkm), full_blocks, init))
    tile_start = pl.multiple_of(i * tq, tq)
    if n_chain > 1:
        later = all_chains[1:]
        carries = steps(later, (tile_start,) * len(later), tuple(c * tc for c in later), carries, None)
    diag_mask = (lax.broadcasted_iota(jnp.int32, (tc, tc), 0) // CHUNK
                 >= lax.broadcasted_iota(jnp.int32, (tc, tc), 1) // CHUNK)
    carries = steps(all_chains, tuple(pl.multiple_of(tile_start + c * tc, tc) for c in all_chains),
                    (tc,) * n_chain, carries, diag_mask)
    for c in range(n_chain):
        acc = carries[c][1]
        o_ref[c * tc:(c + 1) * tc, :] = (acc[:, :MLA_V] / acc[:, MLA_V:MLA_V + 1]).astype(o_ref.dtype)


def _attention(q, k, v):
    b, hd, s, _ = q.shape
    tq = min(TQ, s)
    tc = min(TC_ATTN, tq)
    tkm = min(TK_MAIN, tq)
    return pl.pallas_call(
        functools.partial(_attn_body, tq=tq, tc=tc, tkm=tkm),
        grid=(b, hd, s // tq),
        in_specs=[
            pl.BlockSpec((None, None, tq, QK_PAD), lambda bi, h, i: (bi, h, i, 0)),
            pl.BlockSpec((None, None, s, QK_PAD), lambda bi, h, i: (bi, h, 0, 0)),
            pl.BlockSpec((None, None, s, V_PAD), lambda bi, h, i: (bi, h, 0, 0)),
        ],
        out_specs=pl.BlockSpec((None, tq, MLA_V), lambda bi, h, i: (bi, i, h)),
        out_shape=jax.ShapeDtypeStruct((b, s, MLA_WIDTH), BF16),
        compiler_params=_cparams(("parallel", "parallel", "arbitrary")),
        name="mla_attention",
    )(q, k, v)


def _causal_conv(x, halo_ref, w):
    t = x.shape[0]
    xe = jnp.concatenate([halo_ref[...], x], axis=0)
    halo_ref[...] = x[t - SUBLANES:, :]
    out = xe[SUBLANES:, :] * w[CONV_WIDTH - 1:CONV_WIDTH, :]
    for j in range(CONV_WIDTH - 1):
        shift = CONV_WIDTH - 1 - j
        out = out + xe[SUBLANES - shift:SUBLANES - shift + t, :] * w[j:j + 1, :]
    return out


def _lru_body(lru_ref, cw_ref, cb_ref, wa_ref, ba_ref, wx_ref, bx_ref, lam_ref, on_ref, y_ref, halo_ref, h_ref):
    @pl.when(pl.program_id(1) == 0)
    def _():
        halo_ref[...] = jnp.zeros_like(halo_ref)
        h_ref[...] = jnp.zeros_like(h_ref)

    blk = lru_ref[...].astype(F32)
    ts = blk.shape[0]
    x = blk[:, :LRU_WIDTH]
    gate = blk[:, LRU_WIDTH:]
    xc = _causal_conv(x, halo_ref, cw_ref[...]) + cb_ref[...]
    xb = xc.astype(BF16)
    r = _sigmoid(_dot(xb, wa_ref[...]) + ba_ref[...])
    ig = _sigmoid(_dot(xb, wx_ref[...]) + bx_ref[...])
    log_a = (-LRU_C) * r * _softplus(-lam_ref[...])
    a = jnp.exp(log_a)
    u = jnp.sqrt(-jnp.tanh(log_a) * (a * a + 1.0)) * (ig * xc)

    row = lax.broadcasted_iota(jnp.int32, (ts, LRU_WIDTH), 0)
    d = 1
    while d < ts:
        if d < SUBLANES:
            keep = row >= d
            a_prev = jnp.where(keep, pltpu.roll(a, d, 0), 1.0)
            u_prev = jnp.where(keep, pltpu.roll(u, d, 0), 0.0)
            u = a * u_prev + u
            a = a * a_prev
        else:
            u = jnp.concatenate([u[:d], a[d:] * u[:ts - d] + u[d:]], axis=0)
            a = jnp.concatenate([a[:d], a[d:] * a[:ts - d]], axis=0)
        d *= 2
    h = a * h_ref[...] + u
    h_ref[...] = h[ts - 1:ts, :]
    y = h * jax.nn.gelu(gate, approximate=True)
    y_ref[...] = _rms(y, on_ref[...]).astype(y_ref.dtype)


def _rglru(lru_in, cw, cb, wa, ba, wx, bx, lam, on):
    b, s, _ = lru_in.shape
    ts = min(TS_LRU, s)
    params = (cw, cb, wa, ba, wx, bx, lam, on)
    return pl.pallas_call(
        _lru_body,
        grid=(b, s // ts),
        in_specs=[pl.BlockSpec((None, ts, W_LRU), lambda i, j: (i, j, 0))] + [_const_spec(p.shape) for p in params],
        out_specs=pl.BlockSpec((None, ts, LRU_WIDTH), lambda i, j: (i, j, 0)),
        out_shape=jax.ShapeDtypeStruct((b, s, LRU_WIDTH), BF16),
        scratch_shapes=[pltpu.VMEM((SUBLANES, LRU_WIDTH), F32), pltpu.VMEM((1, LRU_WIDTH), F32)],
        compiler_params=_cparams(("parallel", "arbitrary")),
        name="rglru",
    )(lru_in, *params)


def _gdn_body(qkv_ref, z_ref, ab_ref, cw_ref, alog_ref, dtb_ref, on_ref, y_ref, halo_ref, state_ref):
    @pl.when(pl.program_id(1) == 0)
    def _():
        halo_ref[...] = jnp.zeros_like(halo_ref)
        state_ref[...] = jnp.zeros_like(state_ref)

    x = qkv_ref[...].astype(F32)
    t = x.shape[0]
    c = _causal_conv(x, halo_ref, cw_ref[...])
    c = c * _sigmoid(c)

    ab = ab_ref[...]
    lane = lax.broadcasted_iota(jnp.int32, (t, LANES), 1)
    g = jnp.where(lane < GDN_HEADS, -jnp.exp(alog_ref[...]) * _softplus(ab + dtb_ref[...]), 0.0)
    beta_all = _sigmoid(ab)

    pos = lax.broadcasted_iota(jnp.int32, (t, LANES), 0) % CHUNK
    gc = g
    rc = g
    d = 1
    while d < CHUNK:
        gc = gc + jnp.where(pos >= d, pltpu.roll(gc, d, 0), 0.0)
        rc = rc + jnp.where(pos + d < CHUNK, pltpu.roll(rc, t - d, 0), 0.0)
        d *= 2
    tail = rc - g
    gct = gc.T

    blk = min(GDN_BLK, t)
    n_chunks = t // CHUNK
    per_blk = blk // CHUNK
    ri = lax.broadcasted_iota(jnp.int32, (blk, blk), 0)
    ci = lax.broadcasted_iota(jnp.int32, (blk, blk), 1)
    same = (ri // CHUNK) == (ci // CHUNK)
    causal = jnp.logical_and(same, ri >= ci)
    strict = jnp.logical_and(same, ri > ci)
    eye = jnp.where(ri == ci, 1.0, 0.0)
    col_chunk = lax.broadcasted_iota(jnp.int32, (GDN_DK, blk), 1) // CHUNK

    units = [(sb, h) for sb in range(t // blk) for h in range(GDN_HEADS)]
    qs, ks, kbs, decays, gccs, rhss = [], [], [], [], [], []
    for sb, h in units:
        rows = slice(sb * blk, (sb + 1) * blk)
        q = c[rows, h * GDN_DK:(h + 1) * GDN_DK]
        k = c[rows, GDN_WIDTH + h * GDN_DK:GDN_WIDTH + (h + 1) * GDN_DK]
        v = c[rows, 2 * GDN_WIDTH + h * GDN_DV:2 * GDN_WIDTH + (h + 1) * GDN_DV]
        q = q * (lax.rsqrt(jnp.sum(q * q, axis=-1, keepdims=True) + RMS_EPS) * (GDN_DK ** -0.5))
        k = k * lax.rsqrt(jnp.sum(k * k, axis=-1, keepdims=True) + RMS_EPS)
        beta = beta_all[rows, GDN_HEADS + h:GDN_HEADS + h + 1]
        gcc = gc[rows, h:h + 1]
        kb = k * beta
        qs.append(q)
        ks.append(k)
        kbs.append(kb)
        gccs.append(gcc)
        decays.append(jnp.exp(jnp.where(causal, gcc - gct[h:h + 1, rows], NEG_BIG)))
        rhss.append(jnp.concatenate([v * beta, kb * jnp.exp(gcc)], axis=1).astype(BF16))

    ids = range(len(units))
    kqs = [_dot_nt(jnp.concatenate([kbs[u], qs[u]], axis=0).astype(BF16), ks[u].astype(BF16)) for u in ids]
    a_lows = [jnp.where(strict, kqs[u][:blk] * decays[u], 0.0) for u in ids]
    attns = [kqs[u][blk:] * decays[u] for u in ids]

    base = SUBLANES
    same_base = (ri // base) == (ci // base)
    invs, pows = [], []
    for u in ids:
        d16 = jnp.where(same_base, a_lows[u], 0.0).astype(BF16)
        invs.append(eye - jnp.where(same_base, a_lows[u], 0.0))
        pows.append(_dot(d16, d16))
    for u in ids:
        p16 = pows[u].astype(BF16)
        both = _dot(jnp.concatenate([invs[u].astype(BF16), p16], axis=0), p16)
        invs[u] = invs[u] + both[:blk]
        pows[u] = both[blk:]
    for u in ids:
        invs[u] = invs[u] + _dot(invs[u].astype(BF16), pows[u].astype(BF16))
    b = base
    while b < CHUNK:
        between = jnp.logical_and((ri // (2 * b)) == (ci // (2 * b)), (ri // b) != (ci // b))
        left = [_dot(invs[u].astype(BF16), jnp.where(between, a_lows[u], 0.0).astype(BF16)) for u in ids]
        for u in ids:
            invs[u] = invs[u] - _dot(left[u].astype(BF16), invs[u].astype(BF16))
        b *= 2
    uws = [_dot(invs[u].astype(BF16), rhss[u]).astype(BF16) for u in ids]

    prods = []
    for u, (sb, h) in enumerate(units):
        ke_t = (ks[u] * jnp.exp(tail[sb * blk:(sb + 1) * blk, h:h + 1])).T
        parts = []
        for n in range(per_blk):
            parts.append(jnp.where(col_chunk == n, ke_t, 0.0))
            parts.append(attns[u][n * CHUNK:(n + 1) * CHUNK, :])
        prods.append(_dot(jnp.concatenate(parts, axis=0).astype(BF16), uws[u]))
    terms = [[None] * n_chunks for _ in range(GDN_HEADS)]
    for u, (sb, h) in enumerate(units):
        prod = prods[u]
        qd = qs[u] * jnp.exp(gccs[u])
        for n in range(per_blk):
            r0 = n * (GDN_DK + CHUNK)
            rows = slice(n * CHUNK, (n + 1) * CHUNK)
            terms[h][sb * per_blk + n] = dict(
                lhs=jnp.concatenate([-prod[r0:r0 + GDN_DK, GDN_DV:],
                                     qd[rows] - prod[r0 + GDN_DK:r0 + GDN_DK + CHUNK, GDN_DV:]], axis=0).astype(BF16),
                add_s=prod[r0:r0 + GDN_DK, :GDN_DV],
                add_o=prod[r0 + GDN_DK:r0 + GDN_DK + CHUNK, :GDN_DV],
                dec=jnp.exp(gccs[u][(n + 1) * CHUNK - 1:(n + 1) * CHUNK, :]))

    states = [state_ref[h] for h in range(GDN_HEADS)]
    outs = [[] for _ in range(GDN_HEADS)]
    rows_h = GDN_DK + CHUNK
    for n in range(n_chunks):
        for h0 in range(0, GDN_HEADS, 2):
            pair = (h0, h0 + 1)
            lhs = jnp.concatenate([terms[h][n]["lhs"] for h in pair], axis=0)
            rhs = jnp.concatenate([states[h] for h in pair], axis=1).astype(BF16)
            both = _dot(lhs, rhs)
            for idx, h in enumerate(pair):
                tm = terms[h][n]
                blk = both[idx * rows_h:(idx + 1) * rows_h, idx * GDN_DV:(idx + 1) * GDN_DV]
                outs[h].append(blk[GDN_DK:] + tm["add_o"])
                states[h] = states[h] * tm["dec"] + blk[:GDN_DK] + tm["add_s"]
    on = on_ref[...]
    for h in range(GDN_HEADS):
        state_ref[h] = states[h]
        o = jnp.concatenate(outs[h], axis=0)
        zz = z_ref[:, h * GDN_DV:(h + 1) * GDN_DV].astype(F32)
        y_ref[:, h * GDN_DV:(h + 1) * GDN_DV] = (_rms(o, on) * (zz * _sigmoid(zz))).astype(y_ref.dtype)


def _gdn(qkv, z, ab, cw, alog, dtb, on):
    b, s, _ = qkv.shape
    t = min(T_GDN, s)
    params = (cw, alog, dtb, on)
    return pl.pallas_call(
        _gdn_body,
        grid=(b, s // t),
        in_specs=[
            pl.BlockSpec((None, t, W_QKV), lambda i, j: (i, j, 0)),
            pl.BlockSpec((None, t, W_Z), lambda i, j: (i, j, 0)),
            pl.BlockSpec((None, t, W_AB), lambda i, j: (i, j, 0)),
        ] + [_const_spec(p.shape) for p in params],
        out_specs=pl.BlockSpec((None, t, GDN_WIDTH), lambda i, j: (i, j, 0)),
        out_shape=jax.ShapeDtypeStruct((b, s, GDN_WIDTH), BF16),
        scratch_shapes=[pltpu.VMEM((SUBLANES, W_QKV), F32), pltpu.VMEM((GDN_HEADS, GDN_DK, GDN_DV), F32)],
        compiler_params=_cparams(("parallel", "arbitrary")),
        name="gdn",
    )(qkv, z, ab, *params)


def _outproj_body(*refs, moe):
    if moe:
        x_ref, om_ref, yl_ref, yg_ref, mon_ref, wo_ref, fn_ref, rw_ref, xo_ref, h_ref, route_ref = refs
    else:
        x_ref, om_ref, yl_ref, yg_ref, mon_ref, wo_ref, fn_ref, xo_ref, h_ref = refs
    ym = _rms(om_ref[...].astype(F32), mon_ref[...]).astype(BF16)
    mixed = jnp.concatenate([ym, yl_ref[...], yg_ref[...]], axis=1)
    xn = x_ref[...] + _dot(mixed, wo_ref[...])
    xo_ref[...] = xn
    h = _rms(xn, fn_ref[...])
    h_ref[...] = h.astype(h_ref.dtype)
    if moe:
        tm = h.shape[0]
        h_hi = h.astype(BF16)
        h_lo = (h - h_hi.astype(F32)).astype(BF16)
        rw = rw_ref[...]
        rw_hi = rw.astype(BF16)
        rw_lo = (rw - rw_hi.astype(F32)).astype(BF16)
        hi_part = _dot(h_hi, jnp.concatenate([rw_hi, rw_lo], axis=1))
        logits = hi_part[:, :LANES] + hi_part[:, LANES:] + _dot(h_lo, rw_hi)
        lane = lax.broadcasted_iota(jnp.int32, (tm, LANES), 1)
        logits = jnp.where(lane < N_EXPERTS, logits, -jnp.inf)
        m1 = jnp.max(logits, axis=-1, keepdims=True)
        i1 = jnp.min(jnp.where(logits == m1, lane, LANES), axis=-1, keepdims=True)
        rest = jnp.where(lane == i1, -jnp.inf, logits)
        m2 = jnp.max(rest, axis=-1, keepdims=True)
        i2 = jnp.min(jnp.where(rest == m2, lane, LANES), axis=-1, keepdims=True)
        e2 = jnp.exp(m2 - m1)
        g1 = 1.0 / (1.0 + e2)
        g2 = e2 * g1
        route = jnp.where(lane == 0, i1.astype(F32),
                          jnp.where(lane == 1, i2.astype(F32),
                                    jnp.where(lane == 2, g1, jnp.where(lane == 3, g2, 0.0))))
        route_ref[...] = route


def _outproj(x, om, yl, yg, mon, wo, fn, rw=None):
    b, s, d = x.shape
    tm = min(TM_OUT, s)
    moe = rw is not None
    row = lambda w: pl.BlockSpec((None, tm, w), lambda i, j: (i, j, 0))
    in_specs = [row(d), row(MLA_WIDTH), row(LRU_WIDTH), row(GDN_WIDTH),
                _const_spec(mon.shape), _const_spec(wo.shape), _const_spec(fn.shape)]
    args = [x, om, yl, yg, mon, wo, fn]
    out_specs = [row(d), row(d)]
    out_shape = [jax.ShapeDtypeStruct((b, s, d), F32), jax.ShapeDtypeStruct((b, s, d), F32 if moe else BF16)]
    if moe:
        in_specs.append(_const_spec(rw.shape))
        args.append(rw)
        out_specs.append(row(LANES))
        out_shape.append(jax.ShapeDtypeStruct((b, s, LANES), F32))
    return pl.pallas_call(
        functools.partial(_outproj_body, moe=moe),
        grid=(b, s // tm),
        in_specs=in_specs,
        out_specs=out_specs,
        out_shape=out_shape,
        compiler_params=_cparams(("parallel", "parallel")),
        name="outproj_moe" if moe else "outproj",
    )(*args)


def _dense_ffn_body(x_ref, h_ref, wg_ref, wu_ref, wd_ref, o_ref, a_ref):
    h = h_ref[...]
    ff = wg_ref.shape[1]
    for lo in range(0, ff, FC_DENSE):
        hi = min(lo + FC_DENSE, ff)
        g = _dot(h, wg_ref[:, lo:hi])
        u = _dot(h, wu_ref[:, lo:hi])
        a_ref[:, lo:hi] = (g * _sigmoid(g) * u).astype(BF16)
    o_ref[...] = x_ref[...] + _dot(a_ref[...], wd_ref[...])


def _dense_ffn(x, h, wg, wu, wd):
    n, d = x.shape
    tm = min(TM_FFN, n)
    ff = wg.shape[1]
    resident = lambda shape: pl.BlockSpec(shape, lambda i: (0, 0), pipeline_mode=pl.Buffered(1))
    return pl.pallas_call(
        _dense_ffn_body,
        grid=(n // tm,),
        in_specs=[
            pl.BlockSpec((tm, d), lambda i: (i, 0)),
            pl.BlockSpec((tm, d), lambda i: (i, 0)),
            resident(wg.shape), resident(wu.shape), resident(wd.shape),
        ],
        out_specs=pl.BlockSpec((tm, d), lambda i: (i, 0)),
        out_shape=jax.ShapeDtypeStruct((n, d), F32),
        scratch_shapes=[pltpu.VMEM((tm, ff), BF16)],
        compiler_params=_cparams(("parallel",)),
        name="dense_ffn",
    )(x, h, wg, wu, wd)


def _row_copy(src_hbm, dst_ref, src_row, dst_row, sem):
    return pltpu.make_async_copy(src_hbm.at[pl.ds(src_row, 1), :], dst_ref.at[pl.ds(dst_row, 1), :], sem)


def _wait_rows(src_hbm, dst_ref, sem):
    pltpu.make_async_copy(src_hbm.at[pl.ds(0, dst_ref.shape[0]), :], dst_ref, sem).wait()


def _dispatch_body(pos_ref, zt_ref, h_hbm, xs_hbm, buf_ref, zero_ref, load_sem, row_sem, zero_sem, *, tc, tm):
    i = pl.program_id(0)
    n_steps = pl.num_programs(0)
    slot = i % 2

    @pl.when(i == 0)
    def _zero_padding_tiles():
        zero_ref[...] = jnp.zeros_like(zero_ref)
        for phase in ("start", "wait"):
            for j in range(zt_ref.shape[0]):
                tile = zt_ref[j]

                @pl.when(tile >= 0)
                def _():
                    dst = xs_hbm.at[pl.ds(pl.multiple_of(jnp.maximum(tile, 0) * tm, tm), tm), :]
                    copy = pltpu.make_async_copy(zero_ref, dst, zero_sem)
                    copy.start() if phase == "start" else copy.wait()

    def block_load(step, dst_slot):
        return pltpu.make_async_copy(h_hbm.at[pl.ds(pl.multiple_of(step * tc, tc), tc), :], buf_ref.at[dst_slot],
                                     load_sem.at[dst_slot])

    def wait_rows_out(src_slot):
        for _ in range(TOP_K):
            pltpu.make_async_copy(buf_ref.at[src_slot], xs_hbm.at[pl.ds(0, tc), :], row_sem.at[src_slot]).wait()

    @pl.when(i == 0)
    def _first_load():
        block_load(0, 0).start()

    block_load(i, slot).wait()
    base = i * tc
    for r in range(tc):
        for k in range(TOP_K):
            pltpu.make_async_copy(buf_ref.at[slot, pl.ds(r, 1), :],
                                  xs_hbm.at[pl.ds(pos_ref[(base + r) * TOP_K + k], 1), :],
                                  row_sem.at[slot]).start(priority=k)

    @pl.when(i > 0)
    def _other_slot_free():
        wait_rows_out(1 - slot)

    @pl.when(i + 1 < n_steps)
    def _next_load():
        block_load(i + 1, 1 - slot).start()

    @pl.when(i == n_steps - 1)
    def _drain():
        wait_rows_out(slot)


def _moe_dispatch(pos_flat, zero_tiles, h, n_rows, tm):
    n, d = h.shape
    tc = min(TC_DISPATCH, n)
    grid_spec = pltpu.PrefetchScalarGridSpec(
        num_scalar_prefetch=2,
        grid=(n // tc,),
        in_specs=[pl.BlockSpec(memory_space=pl.ANY)],
        out_specs=pl.BlockSpec(memory_space=pl.ANY),
        scratch_shapes=[pltpu.VMEM((2, tc, d), F32), pltpu.VMEM((tm, d), F32), pltpu.SemaphoreType.DMA((2,)),
                        pltpu.SemaphoreType.DMA((2,)), pltpu.SemaphoreType.DMA(())],
    )
    return pl.pallas_call(
        functools.partial(_dispatch_body, tc=tc, tm=tm),
        grid_spec=grid_spec,
        out_shape=jax.ShapeDtypeStruct((n_rows, d), F32),
        compiler_params=_cparams(("arbitrary",)),
        name="moe_dispatch",
    )(pos_flat, zero_tiles, h)


def _moe_body(te_ref, tv_ref, x_ref, wg_ref, wu_ref, wd_ref, y_ref, a_ref):
    del te_ref
    i = pl.program_id(0)
    f = pl.program_id(1)

    @pl.when(f == 0)
    def _zero():
        y_ref[...] = jnp.zeros_like(y_ref)

    @pl.when(tv_ref[i] > 0)
    def _compute():
        x = x_ref[...].astype(BF16)
        tf = wg_ref.shape[1]
        for lo in range(0, tf, FC_MOE):
            hi = min(lo + FC_MOE, tf)
            g = _dot(x, wg_ref[:, lo:hi])
            u = _dot(x, wu_ref[:, lo:hi])
            a_ref[:, lo:hi] = (g * _sigmoid(g) * u).astype(BF16)
        y_ref[...] += _dot(a_ref[...], wd_ref[...])


def _moe_experts(tile_expert, tile_valid, xs, wg, wu, wd, tm):
    n_rows, d = xs.shape
    n_tiles = tile_expert.shape[0]
    ff = wg.shape[2]
    tf = TF_MOE if ff % TF_MOE == 0 else ff
    nf = ff // tf

    def f_idx(i, f, tv):
        return jnp.where(tv[i] > 0, f, nf - 1)

    grid_spec = pltpu.PrefetchScalarGridSpec(
        num_scalar_prefetch=2,
        grid=(n_tiles, nf),
        in_specs=[
            pl.BlockSpec((tm, d), lambda i, f, te, tv: (i, 0)),
            pl.BlockSpec((None, d, tf), lambda i, f, te, tv: (te[i], 0, f_idx(i, f, tv))),
            pl.BlockSpec((None, d, tf), lambda i, f, te, tv: (te[i], 0, f_idx(i, f, tv))),
            pl.BlockSpec((None, tf, d), lambda i, f, te, tv: (te[i], f_idx(i, f, tv), 0)),
        ],
        out_specs=pl.BlockSpec((tm, d), lambda i, f, te, tv: (i, 0)),
        scratch_shapes=[pltpu.VMEM((tm, tf), BF16)],
    )
    return pl.pallas_call(
        _moe_body,
        grid_spec=grid_spec,
        out_shape=jax.ShapeDtypeStruct((n_rows, d), F32),
        compiler_params=_cparams(("arbitrary", "arbitrary")),
        name="moe_experts",
    )(tile_expert, tile_valid, xs, wg, wu, wd)


def _combine_body(pos_ref, x_ref, route_ref, y_hbm, o_ref, buf_ref, sem, *, tc):
    i = pl.program_id(0)
    slot = i % 2

    def gather_row(step, dst_slot, r):
        for k in range(TOP_K):
            _row_copy(y_hbm, buf_ref.at[dst_slot, k], pos_ref[(step * tc + r) * TOP_K + k], r,
                      sem.at[dst_slot]).start(priority=k)

    @pl.when(i == 0)
    def _first_gather():
        def issue(r, carry):
            gather_row(0, 0, r)
            return carry
        lax.fori_loop(0, tc, issue, 0, unroll=GATHER_UNROLL)

    @pl.when(i + 1 < pl.num_programs(0))
    def _prefetch():
        for r in range(tc):
            gather_row(i + 1, 1 - slot, r)

    for k in range(TOP_K):
        _wait_rows(y_hbm, buf_ref.at[slot, k], sem.at[slot])
    acc = x_ref[...]
    route = route_ref[...]
    for k in range(TOP_K):
        acc = acc + route[:, TOP_K + k:TOP_K + k + 1] * buf_ref[slot, k]
    o_ref[...] = acc


def _moe_combine(pos_flat, x, route, y_sorted):
    n, d = x.shape
    tc = min(TC_COMB, n)
    grid_spec = pltpu.PrefetchScalarGridSpec(
        num_scalar_prefetch=1,
        grid=(n // tc,),
        in_specs=[pl.BlockSpec((tc, d), lambda i, pos: (i, 0)), pl.BlockSpec((tc, LANES), lambda i, pos: (i, 0)),
                  pl.BlockSpec(memory_space=pl.ANY)],
        out_specs=pl.BlockSpec((tc, d), lambda i, pos: (i, 0)),
        scratch_shapes=[pltpu.VMEM((2, TOP_K, tc, d), F32), pltpu.SemaphoreType.DMA((2,))],
    )
    return pl.pallas_call(
        functools.partial(_combine_body, tc=tc),
        grid_spec=grid_spec,
        out_shape=jax.ShapeDtypeStruct((n, d), F32),
        compiler_params=_cparams(("arbitrary",)),
        name="moe_combine",
    )(pos_flat, x, route, y_sorted)


def _moe_ffn(x, h, route, wg, wu, wd):
    n, d = x.shape
    tm = min(TM_MOE, n)
    experts = route[:, :TOP_K].astype(jnp.int32).reshape(-1)
    onehot = (experts[:, None] == jnp.arange(N_EXPERTS, dtype=jnp.int32)[None, :]).astype(jnp.int32)
    rank = jnp.sum((jnp.cumsum(onehot, axis=0) - onehot) * onehot, axis=1)
    counts = jnp.sum(onehot, axis=0)
    padded = ((counts + tm - 1) // tm) * tm
    ends = jnp.cumsum(padded)
    starts = ends - padded
    pos = starts[experts] + rank
    n_tiles = (n * TOP_K) // tm + N_EXPERTS
    tile_start = jnp.arange(n_tiles, dtype=jnp.int32) * tm
    tile_valid = (tile_start < ends[-1]).astype(jnp.int32)
    tile_expert = jnp.minimum(jnp.sum((tile_start[:, None] >= ends[None, :]).astype(jnp.int32), axis=1), N_EXPERTS - 1)
    last_expert = jnp.max(jnp.where(counts > 0, jnp.arange(N_EXPERTS, dtype=jnp.int32), 0))
    tile_expert = jnp.where(tile_valid > 0, tile_expert, last_expert).astype(jnp.int32)
    pos = pos.astype(jnp.int32)
    group_last = jnp.where(counts > 0, ends // tm - 1, -1)
    tail = ends[-1] // tm + jnp.arange(N_EXPERTS, dtype=jnp.int32)
    tail = jnp.where(tail < n_tiles, tail, -1)
    zero_tiles = jnp.concatenate([group_last, tail]).astype(jnp.int32)
    xs = _moe_dispatch(pos, zero_tiles, h, n_tiles * tm, tm)
    y_sorted = _moe_experts(tile_expert, tile_valid, xs, wg, wu, wd, tm)
    return _moe_combine(pos, x, route, y_sorted)


def _pad_cols(w, width):
    return jnp.pad(w, ((0, 0), (0, width - w.shape[1])))


def _prep_w_in(w):
    n_mla = MLA_Q_RANK + MLA_KV_RANK + MLA_ROPE
    n_main = n_mla + W_LRU + W_QKV + W_Z
    return jnp.concatenate([_pad_cols(w[:, :n_mla], W_MLA), w[:, n_mla:n_main], _pad_cols(w[:, n_main:], W_AB)],
                           axis=1).astype(BF16)


def _prep_w_uq(w):
    w = w.reshape(MLA_Q_RANK, MLA_HEADS, MLA_QK_DIM)
    w = jnp.pad(w, ((0, 0), (0, 0), (0, QK_PAD - MLA_QK_DIM)))
    return w.reshape(MLA_Q_RANK, MLA_HEADS * QK_PAD).astype(BF16)


def _block_diag(w):
    g, bi, bo = w.shape
    eye = jnp.eye(g, dtype=w.dtype)
    return (eye[:, None, :, None] * w[:, :, None, :]).reshape(g * bi, g * bo).astype(BF16)


def _row(v, width=None):
    v = v.reshape(1, -1).astype(F32)
    return v if width is None else _pad_cols(v, width)


def kernel(x, positions, mix_norm, w_in, mla_q_norm, mla_w_uq, mla_kv_norm, mla_w_ukv, mla_q_head_norm, mla_k_head_norm, mla_out_norm, lru_conv_w, lru_conv_b, lru_w_a, lru_b_a, lru_w_x, lru_b_x, lru_lambda, lru_out_norm, gdn_conv_w, gdn_a_log, gdn_dt_bias, gdn_out_norm, w_out, ffn_norm, dense_w_gate, dense_w_up, dense_w_down, router_w, moe_w_gate, moe_w_up, moe_w_down):
    b, s, d = x.shape
    depth = w_in.shape[0]
    pos3 = positions.reshape(b, s, 1)
    inv_freq = ROPE_THETA ** (-jnp.arange(0, MLA_ROPE, 2, dtype=F32) / MLA_ROPE)
    freq = _row(jnp.concatenate([inv_freq, inv_freq]), LANES)

    for layer in range(depth):
        mla_in, lru_in, qkv_in, z_in, ab_in = _inproj(x, _row(mix_norm[layer]), _prep_w_in(w_in[layer]))

        q, k, v = _mla_prep(
            mla_in, pos3, freq, _row(mla_q_norm[layer]), _prep_w_uq(mla_w_uq[layer]), _row(mla_kv_norm[layer]),
            mla_w_ukv[layer].astype(BF16), _row(mla_q_head_norm[layer], QK_PAD), _row(mla_k_head_norm[layer], QK_PAD))
        o_mla = _attention(q, k, v)

        y_lru = _rglru(
            lru_in, lru_conv_w[layer], _row(lru_conv_b[layer]), _block_diag(lru_w_a[layer]), _row(lru_b_a[layer]),
            _block_diag(lru_w_x[layer]), _row(lru_b_x[layer]), _row(lru_lambda[layer]), _row(lru_out_norm[layer]))

        y_gdn = _gdn(qkv_in, z_in, ab_in, gdn_conv_w[layer], _row(gdn_a_log[layer], LANES),
                     _row(gdn_dt_bias[layer], LANES), _row(gdn_out_norm[layer]))

        wo = w_out[layer].astype(BF16)
        if layer % 2 == 0:
            x, h = _outproj(x, o_mla, y_lru, y_gdn, _row(mla_out_norm[layer]), wo, _row(ffn_norm[layer]))
            e = layer // 2
            x = _dense_ffn(x.reshape(b * s, d), h.reshape(b * s, d), dense_w_gate[e].astype(BF16),
                           dense_w_up[e].astype(BF16), dense_w_down[e].astype(BF16)).reshape(b, s, d)
        else:
            e = layer // 2
            x, h, route = _outproj(x, o_mla, y_lru, y_gdn, _row(mla_out_norm[layer]), wo, _row(ffn_norm[layer]),
                                   _pad_cols(router_w[e].astype(F32), LANES))
            x = _moe_ffn(x.reshape(b * s, d), h.reshape(b * s, d), route.reshape(b * s, LANES),
                         moe_w_gate[e].astype(BF16), moe_w_up[e].astype(BF16),
                         moe_w_down[e].astype(BF16)).reshape(b, s, d)
    return x
```

```python
import functools
import math

import jax
import jax.numpy as jnp
from jax import lax
from jax.experimental import pallas as pl
from jax.experimental.pallas import tpu as pltpu

F32 = jnp.float32
BF16 = jnp.bfloat16

D_MODEL = 1024
CHUNK = 64
RMS_EPS = 1e-6
CONV_WIDTH = 4
MLA_HEADS = 4
MLA_NOPE = 128
MLA_ROPE = 64
MLA_V = 128
MLA_QK_DIM = MLA_NOPE + MLA_ROPE
MLA_Q_RANK = 512
MLA_KV_RANK = 256
MLA_WIDTH = MLA_HEADS * MLA_V
MLA_SCALE = MLA_QK_DIM ** -0.5
ROPE_THETA = 10000.0
LRU_WIDTH = 512
LRU_BLOCKS = 8
LRU_C = 8.0
GDN_HEADS = 4
GDN_DK = 128
GDN_DV = 128
GDN_WIDTH = GDN_HEADS * GDN_DV
N_EXPERTS = 8
TOP_K = 2

LANES = 128
SUBLANES = 8
QK_PAD = 256
V_PAD = 256
LOG2_E = math.log2(math.e)
VMEM_LIMIT = 56 * 1024 * 1024

W_MLA = MLA_Q_RANK + MLA_KV_RANK + LANES
W_LRU = 2 * LRU_WIDTH
W_QKV = 3 * GDN_WIDTH
W_Z = GDN_WIDTH
W_AB = LANES
IN_GROUPS = (W_MLA, W_LRU, W_QKV, W_Z, W_AB)

TM_PROJ = 512
TM_PREP = 1024
TQ = 2048
TC_ATTN = 512
TK_MAIN = 2048
TS_LRU = 256
T_GDN = 512
GDN_BLK = 256
TM_OUT = 1024
TM_FFN = 1024
TM_FUSED = 512
FC_DENSE = 1024
TM_MOE = 512
TF_MOE = 1792
FC_MOE = 1024
TC_DISPATCH = 1024
TC_COMB = 512
GATHER_UNROLL = 16


def _cparams(sem):
    return pltpu.CompilerParams(dimension_semantics=sem, vmem_limit_bytes=VMEM_LIMIT)


def _rms(x, gain):
    return x * lax.rsqrt(jnp.mean(x * x, axis=-1, keepdims=True) + RMS_EPS) * gain


def _sigmoid(x):
    return 1.0 / (1.0 + jnp.exp(-x))


def _softplus(x):
    return jnp.maximum(x, 0.0) + jnp.log1p(jnp.exp(-jnp.abs(x)))


def _dot(a, b, **kw):
    return jnp.dot(a, b, preferred_element_type=F32, **kw)


def _dot_nt(a, b):
    return lax.dot_general(a, b, (((1,), (1,)), ((), ())), preferred_element_type=F32)


def _const_spec(shape):
    nd = len(shape)
    return pl.BlockSpec(shape, lambda *_: (0,) * nd)


def _inproj_body(x_ref, g_ref, w_ref, mla_ref, lru_ref, qkv_ref, z_ref, ab_ref):
    h = _rms(x_ref[...], g_ref[...]).astype(BF16)
    proj = _dot(h, w_ref[...])
    off = 0
    for ref in (mla_ref, lru_ref, qkv_ref, z_ref, ab_ref):
        width = ref.shape[-1]
        ref[...] = proj[:, off:off + width].astype(ref.dtype)
        off += width


def _inproj(x, gain, w_pad):
    b, s, d = x.shape
    tm = min(TM_PROJ, s)
    dtypes = (BF16, BF16, BF16, BF16, F32)
    return pl.pallas_call(
        _inproj_body,
        grid=(b, s // tm),
        in_specs=[
            pl.BlockSpec((None, tm, d), lambda i, j: (i, j, 0)),
            _const_spec((1, d)),
            _const_spec(w_pad.shape),
        ],
        out_specs=[pl.BlockSpec((None, tm, w), lambda i, j: (i, j, 0)) for w in IN_GROUPS],
        out_shape=[jax.ShapeDtypeStruct((b, s, w), dt) for w, dt in zip(IN_GROUPS, dtypes)],
        compiler_params=_cparams(("parallel", "parallel")),
        name="inproj",
    )(x, gain, w_pad)


def _mla_prep_body(mla_ref, pos_ref, freq_ref, qn_ref, wuq_ref, kvn_ref, wukv_ref, qhn_ref, khn_ref,
                   q_ref, k_ref, v_ref):
    m = mla_ref[...].astype(F32)
    tm = m.shape[0]
    cq = m[:, :MLA_Q_RANK]
    ckv = m[:, MLA_Q_RANK:MLA_Q_RANK + MLA_KV_RANK]
    kr = m[:, MLA_Q_RANK + MLA_KV_RANK:]
    qall = _dot(_rms(cq, qn_ref[...]).astype(BF16), wuq_ref[...])
    kvall = _dot(_rms(ckv, kvn_ref[...]).astype(BF16), wukv_ref[...])

    ang = pos_ref[...].astype(F32) * freq_ref[...]
    cos = jnp.cos(ang)
    sin = jnp.sin(ang)
    lane = lax.broadcasted_iota(jnp.int32, (tm, LANES), 1)
    half = MLA_ROPE // 2
    first = lane < half
    sin_signed = jnp.where(first, -sin, sin)

    def rope(x):
        swapped = jnp.where(first, pltpu.roll(x, LANES - half, 1), pltpu.roll(x, half, 1))
        return x * cos + swapped * sin_signed

    qg = qhn_ref[...]
    kg = khn_ref[...]
    kr_ss = jnp.sum(kr * kr, axis=-1, keepdims=True)
    ones_col = jnp.where(lane == 0, 1.0, 0.0).astype(BF16)
    for h in range(MLA_HEADS):
        base = h * QK_PAD
        qn = qall[:, base:base + MLA_NOPE]
        qr = qall[:, base + MLA_NOPE:base + QK_PAD]
        ss = jnp.sum(qn * qn, axis=-1, keepdims=True) + jnp.sum(qr * qr, axis=-1, keepdims=True)
        r = lax.rsqrt(ss * (1.0 / MLA_QK_DIM) + RMS_EPS) * (MLA_SCALE * LOG2_E)
        q_ref[h, :, :MLA_NOPE] = (qn * r * qg[:, :MLA_NOPE]).astype(BF16)
        q_ref[h, :, MLA_NOPE:] = rope(qr * r * qg[:, MLA_NOPE:]).astype(BF16)

        kn = kvall[:, base:base + MLA_NOPE]
        ss = jnp.sum(kn * kn, axis=-1, keepdims=True) + kr_ss
        r = lax.rsqrt(ss * (1.0 / MLA_QK_DIM) + RMS_EPS)
        k_ref[h, :, :MLA_NOPE] = (kn * r * kg[:, :MLA_NOPE]).astype(BF16)
        k_ref[h, :, MLA_NOPE:] = rope(kr * r * kg[:, MLA_NOPE:]).astype(BF16)
        v_ref[h, :, :MLA_V] = kvall[:, base + MLA_NOPE:base + QK_PAD].astype(BF16)
        v_ref[h, :, MLA_V:] = ones_col


def _mla_prep(mla_in, pos3, freq, qn, wuq, kvn, wukv, qhn, khn):
    b, s, _ = mla_in.shape
    tm = min(TM_PREP, s)
    hd = MLA_HEADS
    return pl.pallas_call(
        _mla_prep_body,
        grid=(b, s // tm),
        in_specs=[
            pl.BlockSpec((None, tm, W_MLA), lambda i, j: (i, j, 0)),
            pl.BlockSpec((None, tm, 1), lambda i, j: (i, j, 0)),
            _const_spec(freq.shape), _const_spec(qn.shape), _const_spec(wuq.shape),
            _const_spec(kvn.shape), _const_spec(wukv.shape), _const_spec(qhn.shape), _const_spec(khn.shape),
        ],
        out_specs=[
            pl.BlockSpec((None, hd, tm, QK_PAD), lambda i, j: (i, 0, j, 0)),
            pl.BlockSpec((None, hd, tm, QK_PAD), lambda i, j: (i, 0, j, 0)),
            pl.BlockSpec((None, hd, tm, V_PAD), lambda i, j: (i, 0, j, 0)),
        ],
        out_shape=[
            jax.ShapeDtypeStruct((b, hd, s, QK_PAD), BF16),
            jax.ShapeDtypeStruct((b, hd, s, QK_PAD), BF16),
            jax.ShapeDtypeStruct((b, hd, s, V_PAD), BF16),
        ],
        compiler_params=_cparams(("parallel", "parallel")),
        name="mla_prep",
    )(mla_in, pos3, freq, qn, wuq, kvn, wukv, qhn, khn)


NEG_BIG = -1e30


def _attn_body(q_ref, k_ref, v_ref, o_ref, *, tq, tc, tkm):
    i = pl.program_id(2)
    n_chain = tq // tc
    qs = [q_ref[c * tc:(c + 1) * tc, :] for c in range(n_chain)]

    def steps(chains, starts, sizes, carries, mask):
        scores = [_dot_nt(qs[c], k_ref[pl.ds(st, sz), :]) for c, st, sz in zip(chains, starts, sizes)]
        probs, m_news = [], []
        for c, s in zip(chains, scores):
            if mask is not None:
                s = jnp.where(mask, s, NEG_BIG)
            m_new = jnp.maximum(carries[c][0], jnp.max(s, axis=-1, keepdims=True))
            probs.append(jnp.exp2(s - m_new).astype(BF16))
            m_news.append(m_new)
        out = list(carries)
        for c, st, sz, p, m_new in zip(chains, starts, sizes, probs, m_news):
            m, acc = carries[c]
            out[c] = (m_new, jnp.exp2(m - m_new) * acc + _dot(p, v_ref[pl.ds(st, sz), :]))
        return out

    all_chains = tuple(range(n_chain))

    def full_blocks(j, carries):
        start = pl.multiple_of(j * tkm, tkm)
        return tuple(steps(all_chains, (start,) * n_chain, (tkm,) * n_chain, carries, None))

    init = tuple((jnp.full((tc, 1), NEG_BIG, F32), jnp.zeros((tc, V_PAD), F32)) for _ in range(n_chain))
    carries = list(lax.fori_loop(0, i * (tq // tkm), full_blocks, init))
    tile_start = pl.multiple_of(i * tq, tq)
    if n_chain > 1:
        later = all_chains[1:]
        carries = steps(later, (tile_start,) * len(later), tuple(c * tc for c in later), carries, None)
    diag_mask = (lax.broadcasted_iota(jnp.int32, (tc, tc), 0) // CHUNK
                 >= lax.broadcasted_iota(jnp.int32, (tc, tc), 1) // CHUNK)
    carries = steps(all_chains, tuple(pl.multiple_of(tile_start + c * tc, tc) for c in all_chains),
                    (tc,) * n_chain, carries, diag_mask)
    for c in range(n_chain):
        acc = carries[c][1]
        o_ref[c * tc:(c + 1) * tc, :] = (acc[:, :MLA_V] / acc[:, MLA_V:MLA_V + 1]).astype(o_ref.dtype)


def _attention(q, k, v):
    b, hd, s, _ = q.shape
    tq = min(TQ, s)
    tc = min(TC_ATTN, tq)
    tkm = min(TK_MAIN, tq)
    return pl.pallas_call(
        functools.partial(_attn_body, tq=tq, tc=tc, tkm=tkm),
        grid=(b, hd, s // tq),
        in_specs=[
            pl.BlockSpec((None, None, tq, QK_PAD), lambda bi, h, i: (bi, h, i, 0)),
            pl.BlockSpec((None, None, s, QK_PAD), lambda bi, h, i: (bi, h, 0, 0)),
            pl.BlockSpec((None, None, s, V_PAD), lambda bi, h, i: (bi, h, 0, 0)),
        ],
        out_specs=pl.BlockSpec((None, tq, MLA_V), lambda bi, h, i: (bi, i, h)),
        out_shape=jax.ShapeDtypeStruct((b, s, MLA_WIDTH), BF16),
        compiler_params=_cparams(("parallel", "parallel", "arbitrary")),
        name="mla_attention",
    )(q, k, v)


def _causal_conv(x, halo_ref, w):
    t = x.shape[0]
    xe = jnp.concatenate([halo_ref[...], x], axis=0)
    halo_ref[...] = x[t - SUBLANES:, :]
    out = xe[SUBLANES:, :] * w[CONV_WIDTH - 1:CONV_WIDTH, :]
    for j in range(CONV_WIDTH - 1):
        shift = CONV_WIDTH - 1 - j
        out = out + xe[SUBLANES - shift:SUBLANES - shift + t, :] * w[j:j + 1, :]
    return out


def _lru_body(lru_ref, cw_ref, cb_ref, wa_ref, ba_ref, wx_ref, bx_ref, lam_ref, on_ref, y_ref, halo_ref, h_ref):
    @pl.when(pl.program_id(1) == 0)
    def _():
        halo_ref[...] = jnp.zeros_like(halo_ref)
        h_ref[...] = jnp.zeros_like(h_ref)

    blk = lru_ref[...].astype(F32)
    ts = blk.shape[0]
    x = blk[:, :LRU_WIDTH]
    gate = blk[:, LRU_WIDTH:]
    xc = _causal_conv(x, halo_ref, cw_ref[...]) + cb_ref[...]
    xb = xc.astype(BF16)
    r = _sigmoid(_dot(xb, wa_ref[...]) + ba_ref[...])
    ig = _sigmoid(_dot(xb, wx_ref[...]) + bx_ref[...])
    log_a = (-LRU_C) * r * _softplus(-lam_ref[...])
    a = jnp.exp(log_a)
    u = jnp.sqrt(-jnp.tanh(log_a) * (a * a + 1.0)) * (ig * xc)

    row = lax.broadcasted_iota(jnp.int32, (ts, LRU_WIDTH), 0)
    d = 1
    while d < ts:
        if d < SUBLANES:
            keep = row >= d
            a_prev = jnp.where(keep, pltpu.roll(a, d, 0), 1.0)
            u_prev = jnp.where(keep, pltpu.roll(u, d, 0), 0.0)
            u = a * u_prev + u
            a = a * a_prev
        else:
            u = jnp.concatenate([u[:d], a[d:] * u[:ts - d] + u[d:]], axis=0)
            a = jnp.concatenate([a[:d], a[d:] * a[:ts - d]], axis=0)
        d *= 2
    h = a * h_ref[...] + u
    h_ref[...] = h[ts - 1:ts, :]
    y = h * jax.nn.gelu(gate, approximate=True)
    y_ref[...] = _rms(y, on_ref[...]).astype(y_ref.dtype)


def _rglru(lru_in, cw, cb, wa, ba, wx, bx, lam, on):
    b, s, _ = lru_in.shape
    ts = min(TS_LRU, s)
    params = (cw, cb, wa, ba, wx, bx, lam, on)
    return pl.pallas_call(
        _lru_body,
        grid=(b, s // ts),
        in_specs=[pl.BlockSpec((None, ts, W_LRU), lambda i, j: (i, j, 0))] + [_const_spec(p.shape) for p in params],
        out_specs=pl.BlockSpec((None, ts, LRU_WIDTH), lambda i, j: (i, j, 0)),
        out_shape=jax.ShapeDtypeStruct((b, s, LRU_WIDTH), BF16),
        scratch_shapes=[pltpu.VMEM((SUBLANES, LRU_WIDTH), F32), pltpu.VMEM((1, LRU_WIDTH), F32)],
        compiler_params=_cparams(("parallel", "arbitrary")),
        name="rglru",
    )(lru_in, *params)


def _gdn_body(qkv_ref, z_ref, ab_ref, cw_ref, alog_ref, dtb_ref, on_ref, y_ref, halo_ref, state_ref):
    @pl.when(pl.program_id(1) == 0)
    def _():
        halo_ref[...] = jnp.zeros_like(halo_ref)
        state_ref[...] = jnp.zeros_like(state_ref)

    x = qkv_ref[...].astype(F32)
    t = x.shape[0]
    c = _causal_conv(x, halo_ref, cw_ref[...])
    c = c * _sigmoid(c)

    ab = ab_ref[...]
    lane = lax.broadcasted_iota(jnp.int32, (t, LANES), 1)
    g = jnp.where(lane < GDN_HEADS, -jnp.exp(alog_ref[...]) * _softplus(ab + dtb_ref[...]), 0.0)
    beta_all = _sigmoid(ab)

    pos = lax.broadcasted_iota(jnp.int32, (t, LANES), 0) % CHUNK
    gc = g
    rc = g
    d = 1
    while d < CHUNK:
        gc = gc + jnp.where(pos >= d, pltpu.roll(gc, d, 0), 0.0)
        rc = rc + jnp.where(pos + d < CHUNK, pltpu.roll(rc, t - d, 0), 0.0)
        d *= 2
    tail = rc - g
    gct = gc.T

    blk = min(GDN_BLK, t)
    n_chunks = t // CHUNK
    per_blk = blk // CHUNK
    ri = lax.broadcasted_iota(jnp.int32, (blk, blk), 0)
    ci = lax.broadcasted_iota(jnp.int32, (blk, blk), 1)
    same = (ri // CHUNK) == (ci // CHUNK)
    causal = jnp.logical_and(same, ri >= ci)
    strict = jnp.logical_and(same, ri > ci)
    eye = jnp.where(ri == ci, 1.0, 0.0)
    col_chunk = lax.broadcasted_iota(jnp.int32, (GDN_DK, blk), 1) // CHUNK

    units = [(sb, h) for sb in range(t // blk) for h in range(GDN_HEADS)]
    qs, ks, kbs, decays, gccs, rhss = [], [], [], [], [], []
    for sb, h in units:
        rows = slice(sb * blk, (sb + 1) * blk)
        q = c[rows, h * GDN_DK:(h + 1) * GDN_DK]
        k = c[rows, GDN_WIDTH + h * GDN_DK:GDN_WIDTH + (h + 1) * GDN_DK]
        v = c[rows, 2 * GDN_WIDTH + h * GDN_DV:2 * GDN_WIDTH + (h + 1) * GDN_DV]
        q = q * (lax.rsqrt(jnp.sum(q * q, axis=-1, keepdims=True) + RMS_EPS) * (GDN_DK ** -0.5))
        k = k * lax.rsqrt(jnp.sum(k * k, axis=-1, keepdims=True) + RMS_EPS)
        beta = beta_all[rows, GDN_HEADS + h:GDN_HEADS + h + 1]
        gcc = gc[rows, h:h + 1]
        kb = k * beta
        qs.append(q)
        ks.append(k)
        kbs.append(kb)
        gccs.append(gcc)
        decays.append(jnp.exp(jnp.where(causal, gcc - gct[h:h + 1, rows], NEG_BIG)))
        rhss.append(jnp.concatenate([v * beta, kb * jnp.exp(gcc)], axis=1).astype(BF16))

    ids = range(len(units))
    kqs = [_dot_nt(jnp.concatenate([kbs[u], qs[u]], axis=0).astype(BF16), ks[u].astype(BF16)) for u in ids]
    a_lows = [jnp.where(strict, kqs[u][:blk] * decays[u], 0.0) for u in ids]
    attns = [kqs[u][blk:] * decays[u] for u in ids]

    base = SUBLANES
    same_base = (ri // base) == (ci // base)
    invs, pows = [], []
    for u in ids:
        d16 = jnp.where(same_base, a_lows[u], 0.0).astype(BF16)
        invs.append(eye - jnp.where(same_base, a_lows[u], 0.0))
        pows.append(_dot(d16, d16))
    for u in ids:
        p16 = pows[u].astype(BF16)
        both = _dot(jnp.concatenate([invs[u].astype(BF16), p16], axis=0), p16)
        invs[u] = invs[u] + both[:blk]
        pows[u] = both[blk:]
    for u in ids:
        invs[u] = invs[u] + _dot(invs[u].astype(BF16), pows[u].astype(BF16))
    b = base
    while b < CHUNK:
        between = jnp.logical_and((ri // (2 * b)) == (ci // (2 * b)), (ri // b) != (ci // b))
        left = [_dot(invs[u].astype(BF16), jnp.where(between, a_lows[u], 0.0).astype(BF16)) for u in ids]
        for u in ids:
            invs[u] = invs[u] - _dot(left[u].astype(BF16), invs[u].astype(BF16))
        b *= 2
    uws = [_dot(invs[u].astype(BF16), rhss[u]).astype(BF16) for u in ids]

    prods = []
    for u, (sb, h) in enumerate(units):
        ke_t = (ks[u] * jnp.exp(tail[sb * blk:(sb + 1) * blk, h:h + 1])).T
        parts = []
        for n in range(per_blk):
            parts.append(jnp.where(col_chunk == n, ke_t, 0.0))
            parts.append(attns[u][n * CHUNK:(n + 1) * CHUNK, :])
        prods.append(_dot(jnp.concatenate(parts, axis=0).astype(BF16), uws[u]))
    terms = [[None] * n_chunks for _ in range(GDN_HEADS)]
    for u, (sb, h) in enumerate(units):
        prod = prods[u]
        qd = qs[u] * jnp.exp(gccs[u])
        for n in range(per_blk):
            r0 = n * (GDN_DK + CHUNK)
            rows = slice(n * CHUNK, (n + 1) * CHUNK)
            terms[h][sb * per_blk + n] = dict(
                lhs=jnp.concatenate([-prod[r0:r0 + GDN_DK, GDN_DV:],
                                     qd[rows] - prod[r0 + GDN_DK:r0 + GDN_DK + CHUNK, GDN_DV:]], axis=0).astype(BF16),
                add_s=prod[r0:r0 + GDN_DK, :GDN_DV],
                add_o=prod[r0 + GDN_DK:r0 + GDN_DK + CHUNK, :GDN_DV],
                dec=jnp.exp(gccs[u][(n + 1) * CHUNK - 1:(n + 1) * CHUNK, :]))

    states = [state_ref[h] for h in range(GDN_HEADS)]
    outs = [[] for _ in range(GDN_HEADS)]
    rows_h = GDN_DK + CHUNK
    for n in range(n_chunks):
        for h0 in range(0, GDN_HEADS, 2):
            pair = (h0, h0 + 1)
            lhs = jnp.concatenate([terms[h][n]["lhs"] for h in pair], axis=0)
            rhs = jnp.concatenate([states[h] for h in pair], axis=1).astype(BF16)
            both = _dot(lhs, rhs)
            for idx, h in enumerate(pair):
                tm = terms[h][n]
                blk = both[idx * rows_h:(idx + 1) * rows_h, idx * GDN_DV:(idx + 1) * GDN_DV]
                outs[h].append(blk[GDN_DK:] + tm["add_o"])
                states[h] = states[h] * tm["dec"] + blk[:GDN_DK] + tm["add_s"]
    on = on_ref[...]
    for h in range(GDN_HEADS):
        state_ref[h] = states[h]
        o = jnp.concatenate(outs[h], axis=0)
        zz = z_ref[:, h * GDN_DV:(h + 1) * GDN_DV].astype(F32)
        y_ref[:, h * GDN_DV:(h + 1) * GDN_DV] = (_rms(o, on) * (zz * _sigmoid(zz))).astype(y_ref.dtype)


def _gdn(qkv, z, ab, cw, alog, dtb, on):
    b, s, _ = qkv.shape
    t = min(T_GDN, s)
    params = (cw, alog, dtb, on)
    return pl.pallas_call(
        _gdn_body,
        grid=(b, s // t),
        in_specs=[
            pl.BlockSpec((None, t, W_QKV), lambda i, j: (i, j, 0)),
            pl.BlockSpec((None, t, W_Z), lambda i, j: (i, j, 0)),
            pl.BlockSpec((None, t, W_AB), lambda i, j: (i, j, 0)),
        ] + [_const_spec(p.shape) for p in params],
        out_specs=pl.BlockSpec((None, t, GDN_WIDTH), lambda i, j: (i, j, 0)),
        out_shape=jax.ShapeDtypeStruct((b, s, GDN_WIDTH), BF16),
        scratch_shapes=[pltpu.VMEM((SUBLANES, W_QKV), F32), pltpu.VMEM((GDN_HEADS, GDN_DK, GDN_DV), F32)],
        compiler_params=_cparams(("parallel", "arbitrary")),
        name="gdn",
    )(qkv, z, ab, *params)


def _outproj_body(*refs, moe):
    if moe:
        x_ref, om_ref, yl_ref, yg_ref, mon_ref, wo_ref, fn_ref, rw_ref, xo_ref, h_ref, route_ref = refs
    else:
        x_ref, om_ref, yl_ref, yg_ref, mon_ref, wo_ref, fn_ref, xo_ref, h_ref = refs
    ym = _rms(om_ref[...].astype(F32), mon_ref[...]).astype(BF16)
    mixed = jnp.concatenate([ym, yl_ref[...], yg_ref[...]], axis=1)
    xn = x_ref[...] + _dot(mixed, wo_ref[...])
    xo_ref[...] = xn
    h = _rms(xn, fn_ref[...])
    h_ref[...] = h.astype(h_ref.dtype)
    if moe:
        tm = h.shape[0]
        h_hi = h.astype(BF16)
        h_lo = (h - h_hi.astype(F32)).astype(BF16)
        rw = rw_ref[...]
        rw_hi = rw.astype(BF16)
        rw_lo = (rw - rw_hi.astype(F32)).astype(BF16)
        hi_part = _dot(h_hi, jnp.concatenate([rw_hi, rw_lo], axis=1))
        logits = hi_part[:, :LANES] + hi_part[:, LANES:] + _dot(h_lo, rw_hi)
        lane = lax.broadcasted_iota(jnp.int32, (tm, LANES), 1)
        logits = jnp.where(lane < N_EXPERTS, logits, -jnp.inf)
        m1 = jnp.max(logits, axis=-1, keepdims=True)
        i1 = jnp.min(jnp.where(logits == m1, lane, LANES), axis=-1, keepdims=True)
        rest = jnp.where(lane == i1, -jnp.inf, logits)
        m2 = jnp.max(rest, axis=-1, keepdims=True)
        i2 = jnp.min(jnp.where(rest == m2, lane, LANES), axis=-1, keepdims=True)
        e2 = jnp.exp(m2 - m1)
        g1 = 1.0 / (1.0 + e2)
        g2 = e2 * g1
        route = jnp.where(lane == 0, i1.astype(F32),
                          jnp.where(lane == 1, i2.astype(F32),
                                    jnp.where(lane == 2, g1, jnp.where(lane == 3, g2, 0.0))))
        route_ref[...] = route


def _outproj(x, om, yl, yg, mon, wo, fn, rw=None):
    b, s, d = x.shape
    tm = min(TM_OUT, s)
    moe = rw is not None
    row = lambda w: pl.BlockSpec((None, tm, w), lambda i, j: (i, j, 0))
    in_specs = [row(d), row(MLA_WIDTH), row(LRU_WIDTH), row(GDN_WIDTH),
                _const_spec(mon.shape), _const_spec(wo.shape), _const_spec(fn.shape)]
    args = [x, om, yl, yg, mon, wo, fn]
    out_specs = [row(d), row(d)]
    out_shape = [jax.ShapeDtypeStruct((b, s, d), F32), jax.ShapeDtypeStruct((b, s, d), F32 if moe else BF16)]
    if moe:
        in_specs.append(_const_spec(rw.shape))
        args.append(rw)
        out_specs.append(row(LANES))
        out_shape.append(jax.ShapeDtypeStruct((b, s, LANES), F32))
    return pl.pallas_call(
        functools.partial(_outproj_body, moe=moe),
        grid=(b, s // tm),
        in_specs=in_specs,
        out_specs=out_specs,
        out_shape=out_shape,
        compiler_params=_cparams(("parallel", "parallel")),
        name="outproj_moe" if moe else "outproj",
    )(*args)


def _outproj_ffn_body(x_ref, om_ref, yl_ref, yg_ref, mon_ref, wo_ref, fn_ref, wg_ref, wu_ref, wd_ref, o_ref, a_ref):
    ym = _rms(om_ref[...].astype(F32), mon_ref[...]).astype(BF16)
    mixed = jnp.concatenate([ym, yl_ref[...], yg_ref[...]], axis=1)
    xn = x_ref[...] + _dot(mixed, wo_ref[...])
    h = _rms(xn, fn_ref[...]).astype(BF16)
    ff = wg_ref.shape[1]
    for lo in range(0, ff, FC_DENSE):
        hi = min(lo + FC_DENSE, ff)
        g = _dot(h, wg_ref[:, lo:hi])
        u = _dot(h, wu_ref[:, lo:hi])
        a_ref[:, lo:hi] = (g * _sigmoid(g) * u).astype(BF16)
    o_ref[...] = xn + _dot(a_ref[...], wd_ref[...])


def _outproj_ffn(x, om, yl, yg, mon, wo, fn, wg, wu, wd):
    b, s, d = x.shape
    tm = min(TM_FUSED, s)
    ff = wg.shape[1]
    row = lambda w: pl.BlockSpec((None, tm, w), lambda i, j: (i, j, 0))
    resident = lambda a: pl.BlockSpec(a.shape, lambda i, j: (0,) * a.ndim, pipeline_mode=pl.Buffered(1))
    return pl.pallas_call(
        _outproj_ffn_body,
        grid=(b, s // tm),
        in_specs=[row(d), row(MLA_WIDTH), row(LRU_WIDTH), row(GDN_WIDTH), _const_spec(mon.shape), resident(wo),
                  _const_spec(fn.shape), resident(wg), resident(wu), resident(wd)],
        out_specs=row(d),
        out_shape=jax.ShapeDtypeStruct((b, s, d), F32),
        scratch_shapes=[pltpu.VMEM((tm, ff), BF16)],
        compiler_params=_cparams(("parallel", "parallel")),
        name="outproj_dense_ffn",
    )(x, om, yl, yg, mon, wo, fn, wg, wu, wd)


def _dense_ffn_body(x_ref, h_ref, wg_ref, wu_ref, wd_ref, o_ref, a_ref):
    h = h_ref[...]
    ff = wg_ref.shape[1]
    for lo in range(0, ff, FC_DENSE):
        hi = min(lo + FC_DENSE, ff)
        g = _dot(h, wg_ref[:, lo:hi])
        u = _dot(h, wu_ref[:, lo:hi])
        a_ref[:, lo:hi] = (g * _sigmoid(g) * u).astype(BF16)
    o_ref[...] = x_ref[...] + _dot(a_ref[...], wd_ref[...])


def _dense_ffn(x, h, wg, wu, wd):
    n, d = x.shape
    tm = min(TM_FFN, n)
    ff = wg.shape[1]
    resident = lambda shape: pl.BlockSpec(shape, lambda i: (0, 0), pipeline_mode=pl.Buffered(1))
    return pl.pallas_call(
        _dense_ffn_body,
        grid=(n // tm,),
        in_specs=[
            pl.BlockSpec((tm, d), lambda i: (i, 0)),
            pl.BlockSpec((tm, d), lambda i: (i, 0)),
            resident(wg.shape), resident(wu.shape), resident(wd.shape),
        ],
        out_specs=pl.BlockSpec((tm, d), lambda i: (i, 0)),
        out_shape=jax.ShapeDtypeStruct((n, d), F32),
        scratch_shapes=[pltpu.VMEM((tm, ff), BF16)],
        compiler_params=_cparams(("parallel",)),
        name="dense_ffn",
    )(x, h, wg, wu, wd)


def _row_copy(src_hbm, dst_ref, src_row, dst_row, sem):
    return pltpu.make_async_copy(src_hbm.at[pl.ds(src_row, 1), :], dst_ref.at[pl.ds(dst_row, 1), :], sem)


def _wait_rows(src_hbm, dst_ref, sem):
    pltpu.make_async_copy(src_hbm.at[pl.ds(0, dst_ref.shape[0]), :], dst_ref, sem).wait()


def _dispatch_body(pos_ref, zt_ref, h_hbm, xs_hbm, buf_ref, zero_ref, load_sem, row_sem, zero_sem, *, tc, tm):
    i = pl.program_id(0)
    n_steps = pl.num_programs(0)
    slot = i % 2

    @pl.when(i == 0)
    def _zero_padding_tiles():
        zero_ref[...] = jnp.zeros_like(zero_ref)
        for phase in ("start", "wait"):
            for j in range(zt_ref.shape[0]):
                tile = zt_ref[j]

                @pl.when(tile >= 0)
                def _():
                    dst = xs_hbm.at[pl.ds(pl.multiple_of(jnp.maximum(tile, 0) * tm, tm), tm), :]
                    copy = pltpu.make_async_copy(zero_ref, dst, zero_sem)
                    copy.start() if phase == "start" else copy.wait()

    def block_load(step, dst_slot):
        return pltpu.make_async_copy(h_hbm.at[pl.ds(pl.multiple_of(step * tc, tc), tc), :], buf_ref.at[dst_slot],
                                     load_sem.at[dst_slot])

    def wait_rows_out(src_slot):
        for _ in range(TOP_K):
            pltpu.make_async_copy(buf_ref.at[src_slot], xs_hbm.at[pl.ds(0, tc), :], row_sem.at[src_slot]).wait()

    @pl.when(i == 0)
    def _first_load():
        block_load(0, 0).start()

    block_load(i, slot).wait()
    base = i * tc
    for r in range(tc):
        for k in range(TOP_K):
            pltpu.make_async_copy(buf_ref.at[slot, pl.ds(r, 1), :],
                                  xs_hbm.at[pl.ds(pos_ref[(base + r) * TOP_K + k], 1), :],
                                  row_sem.at[slot]).start(priority=k)

    @pl.when(i > 0)
    def _other_slot_free():
        wait_rows_out(1 - slot)

    @pl.when(i + 1 < n_steps)
    def _next_load():
        block_load(i + 1, 1 - slot).start()

    @pl.when(i == n_steps - 1)
    def _drain():
        wait_rows_out(slot)


def _moe_dispatch(pos_flat, zero_tiles, h, n_rows, tm):
    n, d = h.shape
    tc = min(TC_DISPATCH, n)
    grid_spec = pltpu.PrefetchScalarGridSpec(
        num_scalar_prefetch=2,
        grid=(n // tc,),
        in_specs=[pl.BlockSpec(memory_space=pl.ANY)],
        out_specs=pl.BlockSpec(memory_space=pl.ANY),
        scratch_shapes=[pltpu.VMEM((2, tc, d), F32), pltpu.VMEM((tm, d), F32), pltpu.SemaphoreType.DMA((2,)),
                        pltpu.SemaphoreType.DMA((2,)), pltpu.SemaphoreType.DMA(())],
    )
    return pl.pallas_call(
        functools.partial(_dispatch_body, tc=tc, tm=tm),
        grid_spec=grid_spec,
        out_shape=jax.ShapeDtypeStruct((n_rows, d), F32),
        compiler_params=_cparams(("arbitrary",)),
        name="moe_dispatch",
    )(pos_flat, zero_tiles, h)


def _moe_body(te_ref, tv_ref, x_ref, wg_ref, wu_ref, wd_ref, y_ref, a_ref):
    del te_ref
    i = pl.program_id(0)
    f = pl.program_id(1)

    @pl.when(f == 0)
    def _zero():
        y_ref[...] = jnp.zeros_like(y_ref)

    @pl.when(tv_ref[i] > 0)
    def _compute():
        x = x_ref[...].astype(BF16)
        tf = wg_ref.shape[1]
        for lo in range(0, tf, FC_MOE):
            hi = min(lo + FC_MOE, tf)
            g = _dot(x, wg_ref[:, lo:hi])
            u = _dot(x, wu_ref[:, lo:hi])
            a_ref[:, lo:hi] = (g * _sigmoid(g) * u).astype(BF16)
        y_ref[...] += _dot(a_ref[...], wd_ref[...])


def _moe_experts(tile_expert, tile_valid, xs, wg, wu, wd, tm):
    n_rows, d = xs.shape
    n_tiles = tile_expert.shape[0]
    ff = wg.shape[2]
    tf = TF_MOE if ff % TF_MOE == 0 else ff
    nf = ff // tf

    def f_idx(i, f, tv):
        return jnp.where(tv[i] > 0, f, nf - 1)

    grid_spec = pltpu.PrefetchScalarGridSpec(
        num_scalar_prefetch=2,
        grid=(n_tiles, nf),
        in_specs=[
            pl.BlockSpec((tm, d), lambda i, f, te, tv: (i, 0)),
            pl.BlockSpec((None, d, tf), lambda i, f, te, tv: (te[i], 0, f_idx(i, f, tv))),
            pl.BlockSpec((None, d, tf), lambda i, f, te, tv: (te[i], 0, f_idx(i, f, tv))),
            pl.BlockSpec((None, tf, d), lambda i, f, te, tv: (te[i], f_idx(i, f, tv), 0)),
        ],
        out_specs=pl.BlockSpec((tm, d), lambda i, f, te, tv: (i, 0)),
        scratch_shapes=[pltpu.VMEM((tm, tf), BF16)],
    )
    return pl.pallas_call(
        _moe_body,
        grid_spec=grid_spec,
        out_shape=jax.ShapeDtypeStruct((n_rows, d), F32),
        compiler_params=_cparams(("arbitrary", "arbitrary")),
        name="moe_experts",
    )(tile_expert, tile_valid, xs, wg, wu, wd)


def _combine_body(pos_ref, x_ref, route_ref, y_hbm, o_ref, buf_ref, sem, *, tc):
    i = pl.program_id(0)
    slot = i % 2

    def gather_row(step, dst_slot, r):
        for k in range(TOP_K):
            _row_copy(y_hbm, buf_ref.at[dst_slot, k], pos_ref[(step * tc + r) * TOP_K + k], r,
                      sem.at[dst_slot]).start(priority=k)

    @pl.when(i == 0)
    def _first_gather():
        def issue(r, carry):
            gather_row(0, 0, r)
            return carry
        lax.fori_loop(0, tc, issue, 0, unroll=GATHER_UNROLL)

    @pl.when(i + 1 < pl.num_programs(0))
    def _prefetch():
        for r in range(tc):
            gather_row(i + 1, 1 - slot, r)

    for k in range(TOP_K):
        _wait_rows(y_hbm, buf_ref.at[slot, k], sem.at[slot])
    acc = x_ref[...]
    route = route_ref[...]
    for k in range(TOP_K):
        acc = acc + route[:, TOP_K + k:TOP_K + k + 1] * buf_ref[slot, k]
    o_ref[...] = acc


def _moe_combine(pos_flat, x, route, y_sorted):
    n, d = x.shape
    tc = min(TC_COMB, n)
    grid_spec = pltpu.PrefetchScalarGridSpec(
        num_scalar_prefetch=1,
        grid=(n // tc,),
        in_specs=[pl.BlockSpec((tc, d), lambda i, pos: (i, 0)), pl.BlockSpec((tc, LANES), lambda i, pos: (i, 0)),
                  pl.BlockSpec(memory_space=pl.ANY)],
        out_specs=pl.BlockSpec((tc, d), lambda i, pos: (i, 0)),
        scratch_shapes=[pltpu.VMEM((2, TOP_K, tc, d), F32), pltpu.SemaphoreType.DMA((2,))],
    )
    return pl.pallas_call(
        functools.partial(_combine_body, tc=tc),
        grid_spec=grid_spec,
        out_shape=jax.ShapeDtypeStruct((n, d), F32),
        compiler_params=_cparams(("arbitrary",)),
        name="moe_combine",
    )(pos_flat, x, route, y_sorted)


def _moe_ffn(x, h, route, wg, wu, wd):
    n, d = x.shape
    tm = min(TM_MOE, n)
    experts = route[:, :TOP_K].astype(jnp.int32).reshape(-1)
    onehot = (experts[:, None] == jnp.arange(N_EXPERTS, dtype=jnp.int32)[None, :]).astype(jnp.int32)
    rank = jnp.sum((jnp.cumsum(onehot, axis=0) - onehot) * onehot, axis=1)
    counts = jnp.sum(onehot, axis=0)
    padded = ((counts + tm - 1) // tm) * tm
    ends = jnp.cumsum(padded)
    starts = ends - padded
    pos = starts[experts] + rank
    n_tiles = (n * TOP_K) // tm + N_EXPERTS
    tile_start = jnp.arange(n_tiles, dtype=jnp.int32) * tm
    tile_valid = (tile_start < ends[-1]).astype(jnp.int32)
    tile_expert = jnp.minimum(jnp.sum((tile_start[:, None] >= ends[None, :]).astype(jnp.int32), axis=1), N_EXPERTS - 1)
    last_expert = jnp.max(jnp.where(counts > 0, jnp.arange(N_EXPERTS, dtype=jnp.int32), 0))
    tile_expert = jnp.where(tile_valid > 0, tile_expert, last_expert).astype(jnp.int32)
    pos = pos.astype(jnp.int32)
    group_last = jnp.where(counts > 0, ends // tm - 1, -1)
    tail = ends[-1] // tm + jnp.arange(N_EXPERTS, dtype=jnp.int32)
    tail = jnp.where(tail < n_tiles, tail, -1)
    zero_tiles = jnp.concatenate([group_last, tail]).astype(jnp.int32)
    xs = _moe_dispatch(pos, zero_tiles, h, n_tiles * tm, tm)
    y_sorted = _moe_experts(tile_expert, tile_valid, xs, wg, wu, wd, tm)
    return _moe_combine(pos, x, route, y_sorted)


def _pad_cols(w, width):
    return jnp.pad(w, ((0, 0), (0, width - w.shape[1])))


def _prep_w_in(w):
    n_mla = MLA_Q_RANK + MLA_KV_RANK + MLA_ROPE
    n_main = n_mla + W_LRU + W_QKV + W_Z
    return jnp.concatenate([_pad_cols(w[:, :n_mla], W_MLA), w[:, n_mla:n_main], _pad_cols(w[:, n_main:], W_AB)],
                           axis=1).astype(BF16)


def _prep_w_uq(w):
    w = w.reshape(MLA_Q_RANK, MLA_HEADS, MLA_QK_DIM)
    w = jnp.pad(w, ((0, 0), (0, 0), (0, QK_PAD - MLA_QK_DIM)))
    return w.reshape(MLA_Q_RANK, MLA_HEADS * QK_PAD).astype(BF16)


def _block_diag(w):
    g, bi, bo = w.shape
    eye = jnp.eye(g, dtype=w.dtype)
    return (eye[:, None, :, None] * w[:, :, None, :]).reshape(g * bi, g * bo).astype(BF16)


def _row(v, width=None):
    v = v.reshape(1, -1).astype(F32)
    return v if width is None else _pad_cols(v, width)


def kernel(x, positions, mix_norm, w_in, mla_q_norm, mla_w_uq, mla_kv_norm, mla_w_ukv, mla_q_head_norm, mla_k_head_norm, mla_out_norm, lru_conv_w, lru_conv_b, lru_w_a, lru_b_a, lru_w_x, lru_b_x, lru_lambda, lru_out_norm, gdn_conv_w, gdn_a_log, gdn_dt_bias, gdn_out_norm, w_out, ffn_norm, dense_w_gate, dense_w_up, dense_w_down, router_w, moe_w_gate, moe_w_up, moe_w_down):
    b, s, d = x.shape
    depth = w_in.shape[0]
    pos3 = positions.reshape(b, s, 1)
    inv_freq = ROPE_THETA ** (-jnp.arange(0, MLA_ROPE, 2, dtype=F32) / MLA_ROPE)
    freq = _row(jnp.concatenate([inv_freq, inv_freq]), LANES)

    for layer in range(depth):
        mla_in, lru_in, qkv_in, z_in, ab_in = _inproj(x, _row(mix_norm[layer]), _prep_w_in(w_in[layer]))

        q, k, v = _mla_prep(
            mla_in, pos3, freq, _row(mla_q_norm[layer]), _prep_w_uq(mla_w_uq[layer]), _row(mla_kv_norm[layer]),
            mla_w_ukv[layer].astype(BF16), _row(mla_q_head_norm[layer], QK_PAD), _row(mla_k_head_norm[layer], QK_PAD))
        o_mla = _attention(q, k, v)

        y_lru = _rglru(
            lru_in, lru_conv_w[layer], _row(lru_conv_b[layer]), _block_diag(lru_w_a[layer]), _row(lru_b_a[layer]),
            _block_diag(lru_w_x[layer]), _row(lru_b_x[layer]), _row(lru_lambda[layer]), _row(lru_out_norm[layer]))

        y_gdn = _gdn(qkv_in, z_in, ab_in, gdn_conv_w[layer], _row(gdn_a_log[layer], LANES),
                     _row(gdn_dt_bias[layer], LANES), _row(gdn_out_norm[layer]))

        wo = w_out[layer].astype(BF16)
        if layer % 2 == 0:
            e = layer // 2
            x = _outproj_ffn(x, o_mla, y_lru, y_gdn, _row(mla_out_norm[layer]), wo, _row(ffn_norm[layer]),
                             dense_w_gate[e].astype(BF16), dense_w_up[e].astype(BF16), dense_w_down[e].astype(BF16))
        else:
            e = layer // 2
            x, h, route = _outproj(x, o_mla, y_lru, y_gdn, _row(mla_out_norm[layer]), wo, _row(ffn_norm[layer]),
                                   _pad_cols(router_w[e].astype(F32), LANES))
            x = _moe_ffn(x.reshape(b * s, d), h.reshape(b * s, d), route.reshape(b * s, LANES),
                         moe_w_gate[e].astype(BF16), moe_w_up[e].astype(BF16),
                         moe_w_down[e].astype(BF16)).reshape(b, s, d)
    return x
```
